```python
import math
import jax, jax.numpy as jnp
from jax import lax
import numpy as np

D_MODEL = 1024
BATCH = 32
SEQ = 2048
DEPTH = 1
DEC_BATCH = 8
DEC_SEQ = 16
PAST_LEN = 2048

CHUNK = 64
QBLOCK = 128
HEAD_DIM = 64
MIX_WIDTH = D_MODEL
DIFF_WIDTH = MIX_WIDTH // 2
SB_WIDTH = MIX_WIDTH - DIFF_WIDTH
DIFF_HEADS = DIFF_WIDTH // (2 * HEAD_DIM)
DIFF_QK_DIM = 2 * HEAD_DIM
DIFF_V_DIM = 2 * HEAD_DIM
SB_HEADS = SB_WIDTH // HEAD_DIM
W_IN_COLS = 3 * DIFF_WIDTH + 3 * SB_WIDTH
D_FF = 2816
CONV_WIDTH = 3
N_BUCKETS = 32
MAX_DISTANCE = 128
EPS = 1e-6

kernel_name = "hybrid_diffattn_stickbreak_convffn_stream_step"


def rms_norm(x, w):
    xf = x.astype(jnp.float32)
    y = xf * lax.rsqrt(jnp.mean(xf * xf, axis=-1, keepdims=True) + EPS)
    return (y * w.astype(jnp.float32)).astype(x.dtype)


def rel_bucket(rel):
    half = N_BUCKETS // 2
    max_exact = half // 2
    ret = jnp.where(rel > 0, half, 0)
    n = jnp.abs(rel)
    nf = jnp.maximum(n, 1).astype(jnp.float32)
    large = max_exact + (jnp.log(nf / max_exact) / math.log(MAX_DISTANCE / max_exact)
                         * (half - max_exact)).astype(jnp.int32)
    large = jnp.minimum(large, half - 1)
    return ret + jnp.where(n < max_exact, n, large)


def _heads(t, n):
    b, s, _ = t.shape
    return t.reshape(b, s, n, -1).transpose(0, 2, 1, 3)


def _merge(t):
    b, h, s, d = t.shape
    return t.transpose(0, 2, 1, 3).reshape(b, s, h * d)


def sweep_queries(block_fn, q, qpos):
    b, h, n_q, d = q.shape
    if n_q <= QBLOCK or n_q % QBLOCK:
        return block_fn(q, qpos)
    nb = n_q // QBLOCK
    qb = q.reshape(b, h, nb, QBLOCK, d).transpose(2, 0, 1, 3, 4)
    pb = qpos.reshape(nb, QBLOCK)
    out = lax.map(lambda a: block_fn(a[0], a[1]), (qb, pb))
    return out.transpose(1, 2, 0, 3, 4).reshape(b, h, n_q, -1)


def diff_attn_block(qb, qpos, k, v, kpos, lam, rel_bias):
    scale = HEAD_DIM ** -0.5
    mask = (kpos[None, :] // CHUNK) <= (qpos[:, None] // CHUNK)
    bias = jnp.transpose(rel_bias[rel_bucket(kpos[None, :] - qpos[:, None])],
                         (2, 0, 1)).astype(jnp.float32)
    qf = qb.astype(jnp.float32) * scale
    kf = k.astype(jnp.float32)
    s1 = jnp.einsum('bhqd,bhkd->bhqk', qf[..., :HEAD_DIM], kf[..., :HEAD_DIM]) + bias
    s2 = jnp.einsum('bhqd,bhkd->bhqk', qf[..., HEAD_DIM:], kf[..., HEAD_DIM:]) + bias
    a = (jax.nn.softmax(jnp.where(mask, s1, -jnp.inf), axis=-1)
         - lam * jax.nn.softmax(jnp.where(mask, s2, -jnp.inf), axis=-1))
    return jnp.einsum('bhqk,bhkd->bhqd', a, v.astype(jnp.float32)).astype(v.dtype)


def stick_breaking_block(qb, qpos, k, v, kpos):
    scale = HEAD_DIM ** -0.5
    z = jnp.einsum('bhqd,bhkd->bhqk', qb.astype(jnp.float32) * scale, k.astype(jnp.float32))
    causal = kpos[None, :] < qpos[:, None]
    log_beta = jax.nn.log_sigmoid(z)
    log_1m = jnp.where(causal, jax.nn.log_sigmoid(-z), 0.0)
    after = lax.cumsum(log_1m, axis=3, reverse=True) - log_1m
    w = jnp.where(causal, jnp.exp(log_beta + after), 0.0)
    return jnp.einsum('bhqk,bhkd->bhqd', w, v.astype(jnp.float32)).astype(v.dtype)


def layer_forward(x, pos, past_dk, past_dv, past_sk, past_sv, past_conv,
                  attn_norm_w, w_in, lq1, lk1, lq2, lk2, subln_w, sb_norm_w, w_out,
                  ffn_norm_w, w_up, conv_w, conv_b, w_down, rel_bias, lam_init):
    b, s, _ = x.shape
    h = rms_norm(x, attn_norm_w)
    proj = jnp.einsum('bsd,de->bse', h, w_in)
    splits = np.cumsum([DIFF_WIDTH, DIFF_WIDTH, DIFF_WIDTH, SB_WIDTH, SB_WIDTH]).tolist()
    q_d, k_d, v_d, q_s, k_s, v_s = jnp.split(proj, splits, axis=-1)
    q_d, k_d, v_d = _heads(q_d, DIFF_HEADS), _heads(k_d, DIFF_HEADS), _heads(v_d, DIFF_HEADS)
    q_s, k_s, v_s = _heads(q_s, SB_HEADS), _heads(k_s, SB_HEADS), _heads(v_s, SB_HEADS)
    if past_dk is None:
        kd_all, vd_all, ks_all, vs_all, kpos = k_d, v_d, k_s, v_s, pos
        conv_prev = jnp.zeros((b, CONV_WIDTH - 1, 2 * D_FF), x.dtype)
    else:
        past_len = past_dk.shape[2]
        kd_all = jnp.concatenate([past_dk.astype(k_d.dtype), k_d], axis=2)
        vd_all = jnp.concatenate([past_dv.astype(v_d.dtype), v_d], axis=2)
        ks_all = jnp.concatenate([past_sk.astype(k_s.dtype), k_s], axis=2)
        vs_all = jnp.concatenate([past_sv.astype(v_s.dtype), v_s], axis=2)
        kpos = jnp.concatenate([jnp.arange(past_len, dtype=jnp.int32), pos])
        conv_prev = past_conv.astype(x.dtype)
    lam = (jnp.exp(jnp.sum(lq1.astype(jnp.float32) * lk1.astype(jnp.float32)))
           - jnp.exp(jnp.sum(lq2.astype(jnp.float32) * lk2.astype(jnp.float32))) + lam_init)
    o_d = sweep_queries(lambda qb, pb: diff_attn_block(qb, pb, kd_all, vd_all, kpos, lam, rel_bias),
                        q_d, pos)
    o_d = rms_norm(o_d, subln_w) * (1.0 - lam_init)
    o_s = sweep_queries(lambda qb, pb: stick_breaking_block(qb, pb, ks_all, vs_all, kpos),
                        q_s, pos)
    o_s = rms_norm(o_s, sb_norm_w)
    mix = jnp.concatenate([_merge(o_d), _merge(o_s)], axis=-1)
    x = x + jnp.einsum('bse,ed->bsd', mix, w_out)
    h2 = rms_norm(x, ffn_norm_w)
    u = jnp.einsum('bsd,df->bsf', h2, w_up)
    padded = jnp.concatenate([conv_prev, u], axis=1)
    c = conv_b
    for i in range(CONV_WIDTH):
        c = c + padded[:, i:i + s] * conv_w[i]
    gate, val = jnp.split(c, 2, axis=-1)
    x = x + jnp.einsum('bsf,fd->bsd', jax.nn.silu(gate) * val, w_down)
    new_conv = padded[:, -(CONV_WIDTH - 1):]
    return x, (k_d, v_d, k_s, v_s, new_conv)


def setup_inputs(seed: int = 0) -> dict:
    key = jax.random.key(seed)
    ks = jax.random.split(key, 24)
    f32 = jnp.float32
    nrm = lambda k, shp, sc: jax.random.normal(k, shp, f32) * sc
    return {
        "x_prompt": nrm(ks[0], (BATCH, SEQ, D_MODEL), 1.0),
        "x_sample": nrm(ks[1], (DEC_BATCH, DEC_SEQ, D_MODEL), 1.0),
        "cache_diff_k": nrm(ks[2], (DEPTH, DEC_BATCH, DIFF_HEADS, PAST_LEN, DIFF_QK_DIM), 1.0),
        "cache_diff_v": nrm(ks[3], (DEPTH, DEC_BATCH, DIFF_HEADS, PAST_LEN, DIFF_V_DIM), 1.0),
        "cache_sb_k": nrm(ks[4], (DEPTH, DEC_BATCH, SB_HEADS, PAST_LEN, HEAD_DIM), 1.0),
        "cache_sb_v": nrm(ks[5], (DEPTH, DEC_BATCH, SB_HEADS, PAST_LEN, HEAD_DIM), 1.0),
        "state_conv": nrm(ks[6], (DEPTH, DEC_BATCH, CONV_WIDTH - 1, 2 * D_FF), 1.0),
        "attn_norm_w": 1.0 + nrm(ks[7], (DEPTH, D_MODEL), 0.02),
        "w_in": nrm(ks[8], (DEPTH, D_MODEL, W_IN_COLS), D_MODEL ** -0.5),
        "lambda_q1": nrm(ks[9], (DEPTH, HEAD_DIM), 0.1),
        "lambda_k1": nrm(ks[10], (DEPTH, HEAD_DIM), 0.1),
        "lambda_q2": nrm(ks[11], (DEPTH, HEAD_DIM), 0.1),
        "lambda_k2": nrm(ks[12], (DEPTH, HEAD_DIM), 0.1),
        "diff_subln_w": 1.0 + nrm(ks[13], (DEPTH, DIFF_V_DIM), 0.02),
        "sb_norm_w": 1.0 + nrm(ks[14], (DEPTH, HEAD_DIM), 0.02),
        "w_out": nrm(ks[15], (DEPTH, MIX_WIDTH, D_MODEL), MIX_WIDTH ** -0.5),
        "ffn_norm_w": 1.0 + nrm(ks[16], (DEPTH, D_MODEL), 0.02),
        "w_up": nrm(ks[17], (DEPTH, D_MODEL, 2 * D_FF), D_MODEL ** -0.5),
        "conv_w": nrm(ks[18], (DEPTH, CONV_WIDTH, 2 * D_FF), CONV_WIDTH ** -0.5),
        "conv_b": nrm(ks[19], (DEPTH, 2 * D_FF), 0.01),
        "w_down": nrm(ks[20], (DEPTH, D_FF, D_MODEL), D_FF ** -0.5),
        "rel_bias": nrm(ks[21], (N_BUCKETS, DIFF_HEADS), 0.5),
        "final_norm_w": 1.0 + nrm(ks[22], (D_MODEL,), 0.02),
    }


def reference(x_prompt, x_sample, cache_diff_k, cache_diff_v, cache_sb_k, cache_sb_v, state_conv,
              attn_norm_w, w_in, lambda_q1, lambda_k1, lambda_q2, lambda_k2, diff_subln_w,
              sb_norm_w, w_out, ffn_norm_w, w_up, conv_w, conv_b, w_down, rel_bias, final_norm_w):
    past_len = cache_diff_k.shape[3]
    pos_p = jnp.arange(x_prompt.shape[1], dtype=jnp.int32)
    pos_s = past_len + jnp.arange(x_sample.shape[1], dtype=jnp.int32)
    xp, xs = x_prompt, x_sample
    p_states, s_states = [], []
    for l in range(DEPTH):
        lam_init = 0.8 - 0.6 * math.exp(-0.3 * l)
        w = (attn_norm_w[l], w_in[l], lambda_q1[l], lambda_k1[l], lambda_q2[l], lambda_k2[l],
             diff_subln_w[l], sb_norm_w[l], w_out[l], ffn_norm_w[l], w_up[l], conv_w[l],
             conv_b[l], w_down[l], rel_bias, lam_init)
        xp, st_p = layer_forward(xp, pos_p, None, None, None, None, None, *w)
        xs, st_s = layer_forward(xs, pos_s, cache_diff_k[l], cache_diff_v[l], cache_sb_k[l],
                                 cache_sb_v[l], state_conv[l], *w)
        p_states.append(st_p)
        s_states.append(st_s)
    y_prompt = rms_norm(xp, final_norm_w)
    y_sample = rms_norm(xs, final_norm_w)
    p_dk, p_dv, p_sk, p_sv, p_cv = [jnp.stack([st[i] for st in p_states], axis=0) for i in range(5)]
    s_dk, s_dv, s_sk, s_sv, s_cv = [jnp.stack([st[i] for st in s_states], axis=0) for i in range(5)]
    return (y_prompt, y_sample, p_dk, p_dv, p_sk, p_sv, p_cv, s_dk, s_dv, s_sk, s_sv, s_cv)
```

```python
import functools
import math

import numpy as np
import jax
import jax.numpy as jnp
from jax import lax
from jax.experimental import pallas as pl
from jax.experimental.pallas import tpu as pltpu

F32 = jnp.float32
BF16 = jnp.bfloat16

HEAD_DIM = 64
LANES = 128
CHUNK = 64
N_BUCKETS = 32
EPS = 1e-6
NEG = -1e30
ATT_TILE = 256
VMEM_LIMIT = 56 * 1024 * 1024


def _rel_bucket_np(rel):
    half = N_BUCKETS // 2
    max_exact = half // 2
    n = np.abs(rel)
    thresholds = (12, 16, 23, 32, 46, 64, 91)
    large = max_exact + sum((n >= t).astype(np.int64) for t in thresholds)
    return (np.where(rel > 0, half, 0) + np.where(n < max_exact, n, large)).astype(np.int32)


def _bucket_tiles(qpos, kpos):
    rel = kpos[None, :] - qpos[:, None]
    vis = (kpos[None, :] // CHUNK) <= (qpos[:, None] // CHUNK)
    return np.where(vis, _rel_bucket_np(rel), -1).astype(np.int32)


def _nt_dot(a, b):
    return lax.dot_general(a, b, (((1,), (1,)), ((), ())), preferred_element_type=F32)


def _dot(a, b):
    return jnp.dot(a, b, preferred_element_type=F32)


def _rms(x, w):
    return x * lax.rsqrt(jnp.mean(x * x, axis=-1, keepdims=True) + EPS) * w


def _prep_kernel(rb_ref, lq1_ref, lk1_ref, lq2_ref, lk2_ref, bp_ref, bsp_ref, bsn_ref,
                 bias_p_ref, bias_sp_ref, bias_sn_ref, lam_ref, *, lam_init):
    h = pl.program_id(0)
    pairs = ((bp_ref, bias_p_ref), (bsp_ref, bias_sp_ref), (bsn_ref, bias_sn_ref))
    for _, out in pairs:
        out[...] = jnp.full(out.shape, NEG, F32)

    def body(b, carry):
        val = rb_ref[b, h]
        for idx, out in pairs:
            out[0] = jnp.where(idx[...] == b, val, out[0])
        return carry

    lax.fori_loop(0, N_BUCKETS, body, 0)
    s1 = jnp.sum(lq1_ref[...] * lk1_ref[...], axis=-1, keepdims=True)
    s2 = jnp.sum(lq2_ref[...] * lk2_ref[...], axis=-1, keepdims=True)
    lam_ref[...] = jnp.exp(s1) - jnp.exp(s2) + lam_init


def _prep(rel_bias, lq1, lk1, lq2, lk2, bkt_p, bkt_sp, bkt_sn, lam_init):
    nh = rel_bias.shape[1]
    full = lambda a: pl.BlockSpec(a.shape, lambda h: (0,) * a.ndim)
    head = lambda a: pl.BlockSpec((1,) + a.shape, lambda h: (h,) + (0,) * a.ndim)
    vec = pl.BlockSpec((1, HEAD_DIM), lambda h: (0, 0))
    return pl.pallas_call(
        functools.partial(_prep_kernel, lam_init=lam_init),
        grid=(nh,),
        in_specs=[pl.BlockSpec(memory_space=pltpu.SMEM), vec, vec, vec, vec,
                  full(bkt_p), full(bkt_sp), full(bkt_sn)],
        out_specs=[head(bkt_p), head(bkt_sp), head(bkt_sn), pl.BlockSpec((1, 1), lambda h: (0, 0))],
        out_shape=[jax.ShapeDtypeStruct((nh,) + bkt_p.shape, F32),
                   jax.ShapeDtypeStruct((nh,) + bkt_sp.shape, F32),
                   jax.ShapeDtypeStruct((nh,) + bkt_sn.shape, F32),
                   jax.ShapeDtypeStruct((1, 1), F32)],
        name="prep",
    )(rel_bias, lq1, lk1, lq2, lk2, bkt_p, bkt_sp, bkt_sn)


def _inproj_kernel(x_ref, nw_ref, w_ref, proj_ref, dk_ref, dv_ref, sk_ref, sv_ref, *, width):
    h = _rms(x_ref[0], nw_ref[...]).astype(BF16)
    n_dh = width // LANES
    n_sh = width // HEAD_DIM
    for g in range(6):
        acc = _dot(h, w_ref[:, g * width:(g + 1) * width])
        proj_ref[0, :, g * width:(g + 1) * width] = acc.astype(BF16)
        if g in (1, 2):
            out = dk_ref if g == 1 else dv_ref
            for hh in range(n_dh):
                out[0, 0, hh] = acc[:, hh * LANES:(hh + 1) * LANES]
        if g in (4, 5):
            out = sk_ref if g == 4 else sv_ref
            for hh in range(n_sh):
                out[0, 0, hh] = acc[:, hh * HEAD_DIM:(hh + 1) * HEAD_DIM]


def _inproj(x, nw, w_in_bf, ts):
    b, s, d = x.shape
    cols = w_in_bf.shape[1]
    width = cols // 6
    n_dh, n_sh = width // LANES, width // HEAD_DIM
    kv = lambda nh, hd: pl.BlockSpec((1, 1, nh, ts, hd), lambda i, j: (0, i, 0, j, 0))
    return pl.pallas_call(
        functools.partial(_inproj_kernel, width=width),
        grid=(b, s // ts),
        in_specs=[pl.BlockSpec((1, ts, d), lambda i, j: (i, j, 0)),
                  pl.BlockSpec((1, d), lambda i, j: (0, 0)),
                  pl.BlockSpec((d, cols), lambda i, j: (0, 0))],
        out_specs=[pl.BlockSpec((1, ts, cols), lambda i, j: (i, j, 0)),
                   kv(n_dh, LANES), kv(n_dh, LANES), kv(n_sh, HEAD_DIM), kv(n_sh, HEAD_DIM)],
        out_shape=[jax.ShapeDtypeStruct((b, s, cols), BF16),
                   jax.ShapeDtypeStruct((1, b, n_dh, s, LANES), F32),
                   jax.ShapeDtypeStruct((1, b, n_dh, s, LANES), F32),
                   jax.ShapeDtypeStruct((1, b, n_sh, s, HEAD_DIM), F32),
                   jax.ShapeDtypeStruct((1, b, n_sh, s, HEAD_DIM), F32)],
        compiler_params=pltpu.CompilerParams(vmem_limit_bytes=VMEM_LIMIT),
        name="inproj",
    )(x, nw, w_in_bf)


def _softmax_step(qm, k, v, bias, m, l, a):
    s = _nt_dot(qm, k) + bias
    mn = jnp.maximum(m, jnp.max(s, axis=-1, keepdims=True))
    alpha = jnp.exp(m - mn)
    p = jnp.exp(s - mn)
    l = alpha * l + jnp.sum(p, axis=-1, keepdims=True)
    a = alpha * a + _dot(p.astype(BF16), v)
    return mn, l, a


def _sb_block(qm, k, v, carry, tri, cmask):
    z = _nt_dot(qm, k)
    sp = jnp.log1p(jnp.exp(-jnp.abs(z)))
    lb = jnp.minimum(z, 0.0) - sp
    l1m = jnp.minimum(-z, 0.0) - sp
    if cmask is not None:
        l1m = jnp.where(cmask, l1m, 0.0)
    hi = l1m.astype(BF16)
    lo = (l1m - hi.astype(F32)).astype(BF16)
    after = _dot(hi, tri) + _dot(lo, tri) + carry
    w = jnp.exp(lb + after)
    if cmask is not None:
        w = jnp.where(cmask, w, 0.0)
    contrib = _dot(w.astype(BF16), v)
    return contrib, carry + jnp.sum(l1m, axis=-1, keepdims=True)


def _diff_finish(a1, l1, a2, l2, lam, subw):
    o = a1 / l1 - lam * (a2 / l2)
    return _rms(o, subw)


def _pair_rms(o, lane, w):
    o2 = o * o
    tot = jnp.sum(o2, axis=-1, keepdims=True)
    lo = jnp.sum(jnp.where(lane < HEAD_DIM, o2, 0.0), axis=-1, keepdims=True)
    ms = jnp.where(lane < HEAD_DIM, lo, tot - lo) * (1.0 / HEAD_DIM)
    return o * lax.rsqrt(ms + EPS) * w


def _diff_kernel(lam_ref, q_ref, k_ref, v_ref, bias_ref, subw_ref, o_ref, *, tile, nq):
    lam = lam_ref[0, 0]
    lane = lax.broadcasted_iota(jnp.int32, (tile, LANES), 1)
    subw = subw_ref[...]

    def q_body(qi, carry):
        q0 = pl.multiple_of(qi * tile, tile)
        q = q_ref[0, pl.ds(q0, tile), :] * jnp.asarray(HEAD_DIM ** -0.5, BF16)
        q1 = jnp.where(lane < HEAD_DIM, q, jnp.zeros_like(q))
        q2 = jnp.where(lane >= HEAD_DIM, q, jnp.zeros_like(q))

        def kb_body(kb, c):
            m1, l1, a1, m2, l2, a2 = c
            k0 = pl.multiple_of(kb * tile, tile)
            k = k_ref[0, pl.ds(k0, tile), :]
            v = v_ref[0, pl.ds(k0, tile), :]
            bias = bias_ref[0, jnp.minimum(qi - kb, 2)]
            m1, l1, a1 = _softmax_step(q1, k, v, bias, m1, l1, a1)
            m2, l2, a2 = _softmax_step(q2, k, v, bias, m2, l2, a2)
            return m1, l1, a1, m2, l2, a2

        col = lambda val: jnp.full((tile, 1), val, F32)
        acc0 = jnp.zeros((tile, LANES), F32)
        init = (col(NEG), col(0.0), acc0, col(NEG), col(0.0), acc0)
        _, l1, a1, _, l2, a2 = lax.fori_loop(0, qi + 1, kb_body, init)
        o_ref[0, pl.ds(q0, tile), :] = _diff_finish(a1, l1, a2, l2, lam, subw).astype(BF16)
        return carry

    lax.fori_loop(0, nq, q_body, 0)


def _diff_attn(lam, proj, bias_p, subw, n_heads):
    b, s, _ = proj.shape
    tile = ATT_TILE
    blk = lambda off: pl.BlockSpec((1, s, LANES), lambda i, h: (i, 0, off + h))
    return pl.pallas_call(
        functools.partial(_diff_kernel, tile=tile, nq=s // tile),
        grid=(b, n_heads),
        in_specs=[pl.BlockSpec(memory_space=pltpu.SMEM),
                  blk(0), blk(n_heads), blk(2 * n_heads),
                  pl.BlockSpec((1, 3, tile, tile), lambda i, h: (h, 0, 0, 0)),
                  pl.BlockSpec((1, LANES), lambda i, h: (0, 0))],
        out_specs=pl.BlockSpec((1, s, LANES), lambda i, h: (i, 0, h)),
        out_shape=jax.ShapeDtypeStruct((b, s, n_heads * LANES), BF16),
        compiler_params=pltpu.CompilerParams(vmem_limit_bytes=VMEM_LIMIT),
        name="diff_attn",
    )(lam, proj, proj, proj, bias_p, subw)


def _sb_kernel(q_ref, k_ref, v_ref, tri_ref, nw_ref, o_ref, *, tile, nq):
    lane = lax.broadcasted_iota(jnp.int32, (tile, LANES), 1)
    row = lax.broadcasted_iota(jnp.int32, (tile, tile), 0)
    colv = lax.broadcasted_iota(jnp.int32, (tile, tile), 1)
    cmask = colv < row
    tri = tri_ref[...]
    nw = nw_ref[...]

    def q_body(qi, carry):
        q0 = pl.multiple_of(qi * tile, tile)
        q = q_ref[0, pl.ds(q0, tile), :] * jnp.asarray(HEAD_DIM ** -0.5, BF16)
        halves = []
        for hh in range(2):
            in_half = (lane >= hh * HEAD_DIM) & (lane < (hh + 1) * HEAD_DIM)
            qm = jnp.where(in_half, q, jnp.zeros_like(q))
            acc, run = _sb_block(qm, k_ref[0, pl.ds(q0, tile), :], v_ref[0, pl.ds(q0, tile), :],
                                 jnp.zeros((tile, 1), F32), tri, cmask)

            def kb_body(it, c):
                acc, run = c
                k0 = pl.multiple_of((qi - 1 - it) * tile, tile)
                contrib, run = _sb_block(qm, k_ref[0, pl.ds(k0, tile), :],
                                         v_ref[0, pl.ds(k0, tile), :], run, tri, None)
                return acc + contrib, run

            acc, _ = lax.fori_loop(0, qi, kb_body, (acc, run))
            halves.append(acc)
        o = jnp.where(lane < HEAD_DIM, halves[0], halves[1])
        o_ref[0, pl.ds(q0, tile), :] = _pair_rms(o, lane, nw).astype(BF16)
        return carry

    lax.fori_loop(0, nq, q_body, 0)


def _sb_attn(proj, tri, nw2, n_pairs, off):
    b, s, _ = proj.shape
    tile = ATT_TILE
    blk = lambda o: pl.BlockSpec((1, s, LANES), lambda i, p: (i, 0, o + p))
    return pl.pallas_call(
        functools.partial(_sb_kernel, tile=tile, nq=s // tile),
        grid=(b, n_pairs),
        in_specs=[blk(off), blk(off + n_pairs), blk(off + 2 * n_pairs),
                  pl.BlockSpec((tile, tile), lambda i, p: (0, 0)),
                  pl.BlockSpec((1, LANES), lambda i, p: (0, 0))],
        out_specs=pl.BlockSpec((1, s, LANES), lambda i, p: (i, 0, p)),
        out_shape=jax.ShapeDtypeStruct((b, s, n_pairs * LANES), BF16),
        compiler_params=pltpu.CompilerParams(vmem_limit_bytes=VMEM_LIMIT),
        name="sb_attn",
    )(proj, proj, proj, tri, nw2)


def _sample_attn_kernel(lam_ref, qd_ref, kd_ref, vd_ref, qs_ref, ks_ref, vs_ref,
                        cdk_ref, cdv_ref, csk_ref, csv_ref, bsp_ref, bsn_ref,
                        subw_ref, nw_ref, tri_ref, od_ref, os_ref, *, sq, past, tile):
    lam = lam_ref[0, 0]
    scale = jnp.asarray(HEAD_DIM ** -0.5, BF16)
    lane = lax.broadcasted_iota(jnp.int32, (sq, LANES), 1)

    q = qd_ref[0] * scale
    kc = cdk_ref[0, 0, 0].astype(BF16)
    vc = cdv_ref[0, 0, 0].astype(BF16)
    kn = kd_ref[0]
    vn = vd_ref[0]
    bias_c = bsp_ref[0]
    bias_n = bsn_ref[0]
    outs = []
    for qm in (jnp.where(lane < HEAD_DIM, q, jnp.zeros_like(q)),
               jnp.where(lane >= HEAD_DIM, q, jnp.zeros_like(q))):
        sc = _nt_dot(qm, kc) + bias_c
        sn = _nt_dot(qm, kn) + bias_n
        m = jnp.maximum(jnp.max(sc, axis=-1, keepdims=True), jnp.max(sn, axis=-1, keepdims=True))
        pc = jnp.exp(sc - m)
        pn = jnp.exp(sn - m)
        l = jnp.sum(pc, axis=-1, keepdims=True) + jnp.sum(pn, axis=-1, keepdims=True)
        outs.append((_dot(pc.astype(BF16), vc) + _dot(pn.astype(BF16), vn), l))
    (a1, l1), (a2, l2) = outs
    od_ref[0] = _diff_finish(a1, l1, a2, l2, lam, subw_ref[...]).astype(BF16)

    row = lax.broadcasted_iota(jnp.int32, (sq, sq), 0)
    colv = lax.broadcasted_iota(jnp.int32, (sq, sq), 1)
    cmask = colv < row
    tri = tri_ref[...]
    qs = qs_ref[0] * scale
    halves = []
    for hh in range(2):
        sl = slice(hh * HEAD_DIM, (hh + 1) * HEAD_DIM)
        qh = qs[:, sl]
        acc, run = _sb_block(qh, ks_ref[0][:, sl], vs_ref[0][:, sl],
                             jnp.zeros((sq, 1), F32), tri[:sq, :sq], cmask)
        for kb in range(past // tile - 1, -1, -1):
            kblk = csk_ref[0, 0, hh, kb * tile:(kb + 1) * tile, :].astype(BF16)
            vblk = csv_ref[0, 0, hh, kb * tile:(kb + 1) * tile, :].astype(BF16)
            contrib, run = _sb_block(qh, kblk, vblk, run, tri, None)
            acc = acc + contrib
        halves.append(acc)
    o = jnp.concatenate(halves, axis=-1)
    os_ref[0] = _pair_rms(o, lane, nw_ref[...]).astype(BF16)


def _sample_attn(lam, proj, cdk, cdv, csk, csv, bias_sp, bias_sn, subw, nw2, tri, n_dh):
    b, sq, _ = proj.shape
    past = cdk.shape[3]
    tile = ATT_TILE
    blk = lambda off: pl.BlockSpec((1, sq, LANES), lambda i, h: (i, 0, off + h))
    cd = pl.BlockSpec((1, 1, 1, past, LANES), lambda i, h: (0, i, h, 0, 0))
    cs = pl.BlockSpec((1, 1, 2, past, HEAD_DIM), lambda i, h: (0, i, h, 0, 0))
    const2 = lambda a: pl.BlockSpec(a.shape, lambda i, h: (0, 0))
    out = pl.BlockSpec((1, sq, LANES), lambda i, h: (i, 0, h))
    return pl.pallas_call(
        functools.partial(_sample_attn_kernel, sq=sq, past=past, tile=tile),
        grid=(b, n_dh),
        in_specs=[pl.BlockSpec(memory_space=pltpu.SMEM),
                  blk(0), blk(n_dh), blk(2 * n_dh), blk(3 * n_dh), blk(4 * n_dh), blk(5 * n_dh),
                  cd, cd, cs, cs,
                  pl.BlockSpec((1, sq, past), lambda i, h: (h, 0, 0)),
                  pl.BlockSpec((1, sq, sq), lambda i, h: (h, 0, 0)),
                  const2(subw), const2(nw2), const2(tri)],
        out_specs=[out, out],
        out_shape=[jax.ShapeDtypeStruct((b, sq, n_dh * LANES), BF16)] * 2,
        compiler_params=pltpu.CompilerParams(vmem_limit_bytes=VMEM_LIMIT),
        name="sample_attn",
    )(lam, proj, proj, proj, proj, proj, proj, cdk, cdv, csk, csv, bias_sp, bias_sn, subw, nw2, tri)


CARRY_ROWS = 8


def _mlp_kernel(x_ref, od_ref, os_ref, cinit_ref, wod_ref, wos_ref, fnw_ref, wup_ref, cw_ref, cb_ref,
                wdn_ref, finw_ref, y_ref, nconv_ref, carry_ref, *, ts, ff, fc):
    @pl.when(pl.program_id(1) == 0)
    def _():
        carry_ref[...] = cinit_ref[0]

    x1 = x_ref[0] + _dot(od_ref[0], wod_ref[...]) + _dot(os_ref[0], wos_ref[...])
    h2 = _rms(x1, fnw_ref[...]).astype(BF16)
    row = lax.broadcasted_iota(jnp.int32, (ts, fc), 0)

    def conv(c0):
        cols = slice(c0, c0 + fc)
        u = _dot(h2, wup_ref[:, cols])
        prev0 = carry_ref[CARRY_ROWS - 2:CARRY_ROWS - 1, cols]
        prev1 = carry_ref[CARRY_ROWS - 1:CARRY_ROWS, cols]
        u1 = jnp.where(row == 0, prev1, pltpu.roll(u, 1, 0))
        u2 = jnp.where(row == 0, prev0, jnp.where(row == 1, prev1, pltpu.roll(u, 2, 0)))
        out = cb_ref[:, cols] + u2 * cw_ref[0:1, cols] + u1 * cw_ref[1:2, cols] + u * cw_ref[2:3, cols]
        tail = u[ts - CARRY_ROWS:, :]
        carry_ref[:, cols] = tail
        nconv_ref[0, :, cols] = tail
        return out

    acc = jnp.zeros_like(x1)
    for c in range(ff // fc):
        g = conv(c * fc)
        v = conv(ff + c * fc)
        act = (g / (1.0 + jnp.exp(-g)) * v).astype(BF16)
        acc = acc + _dot(act, wdn_ref[c * fc:(c + 1) * fc, :])
    y_ref[0] = _rms(x1 + acc, finw_ref[...])


def _mlp(x, od, osb, cinit, wod, wos, fnw, wup, cw, cb, wdn, finw, ts):
    b, s, d = x.shape
    ff = wdn.shape[0]
    half = od.shape[2]
    fc = 256
    const = lambda a: pl.BlockSpec(a.shape, lambda i, j: (0, 0))
    tok = lambda w: pl.BlockSpec((1, ts, w), lambda i, j: (i, j, 0))
    state = pl.BlockSpec((1, CARRY_ROWS, 2 * ff), lambda i, j: (i, 0, 0))
    return pl.pallas_call(
        functools.partial(_mlp_kernel, ts=ts, ff=ff, fc=fc),
        grid=(b, s // ts),
        in_specs=[tok(d), tok(half), tok(half), state, const(wod), const(wos), const(fnw), const(wup),
                  const(cw), const(cb), const(wdn), const(finw)],
        out_specs=[tok(d), state],
        out_shape=[jax.ShapeDtypeStruct((b, s, d), F32),
                   jax.ShapeDtypeStruct((b, CARRY_ROWS, 2 * ff), F32)],
        scratch_shapes=[pltpu.VMEM((CARRY_ROWS, 2 * ff), F32)],
        compiler_params=pltpu.CompilerParams(dimension_semantics=("arbitrary", "arbitrary"),
                                             vmem_limit_bytes=VMEM_LIMIT),
        name="mlp",
    )(x, od, osb, cinit, wod, wos, fnw, wup, cw, cb, wdn, finw)


def kernel(x_prompt, x_sample, cache_diff_k, cache_diff_v, cache_sb_k, cache_sb_v, state_conv, attn_norm_w, w_in, lambda_q1, lambda_k1, lambda_q2, lambda_k2, diff_subln_w, sb_norm_w, w_out, ffn_norm_w, w_up, conv_w, conv_b, w_down, rel_bias, final_norm_w):
    depth = w_in.shape[0]
    assert depth == 1, "single-layer step"
    bp, sp, d = x_prompt.shape
    bs, ss, _ = x_sample.shape
    past = cache_diff_k.shape[3]
    n_dh = cache_diff_k.shape[2]
    n_sh = cache_sb_k.shape[2]
    ff = w_down.shape[1]
    tile = ATT_TILE
    assert sp % tile == 0 and past % tile == 0 and tile % CHUNK == 0 and n_sh == 2 * n_dh
    assert ss % CARRY_ROWS == 0
    lam_init = 0.8 - 0.6 * math.exp(-0.3 * 0)

    loc = np.arange(tile)
    bkt_p = np.stack([_bucket_tiles(loc + tile, loc + tile), _bucket_tiles(loc + tile, loc),
                      _bucket_tiles(loc + 2 * tile, loc)])
    assert (bkt_p[2] == bkt_p[2, 0, 0]).all()
    qpos_s = past + np.arange(ss)
    bkt_sp = _bucket_tiles(qpos_s, np.arange(past))
    bkt_sn = _bucket_tiles(qpos_s, qpos_s)

    row = lambda v: v.reshape(1, -1).astype(F32)
    bias_p, bias_sp, bias_sn, lam = _prep(
        rel_bias.astype(F32), row(lambda_q1[0]), row(lambda_k1[0]), row(lambda_q2[0]), row(lambda_k2[0]),
        jnp.asarray(bkt_p), jnp.asarray(bkt_sp), jnp.asarray(bkt_sn), lam_init)

    w_in_bf = w_in[0].astype(BF16)
    w_out_bf = w_out[0].astype(BF16)
    half = n_dh * LANES
    wod, wos = w_out_bf[:half], w_out_bf[half:]
    wup = w_up[0].astype(BF16)
    wdn = w_down[0].astype(BF16)
    anw, fnw, finw = row(attn_norm_w[0]), row(ffn_norm_w[0]), row(final_norm_w)
    subw = row(diff_subln_w[0]) * (1.0 - lam_init)
    nw2 = jnp.tile(row(sb_norm_w[0]), (1, 2))
    cw, cb = conv_w[0].astype(F32), row(conv_b[0])
    tri = jnp.asarray(np.triu(np.ones((tile, tile), np.float32), 1).T, BF16)

    def state8(st):
        return jnp.pad(st, ((0, 0), (CARRY_ROWS - st.shape[1], 0), (0, 0)))

    ts_p = min(512, sp)
    proj_p, p_dk, p_dv, p_sk, p_sv = _inproj(x_prompt, anw, w_in_bf, ts_p)
    od_p = _diff_attn(lam, proj_p, bias_p, subw, n_dh)
    os_p = _sb_attn(proj_p, tri, nw2, n_dh, 3 * n_dh)
    y_p, cv_p = _mlp(x_prompt, od_p, os_p, jnp.zeros((bp, CARRY_ROWS, 2 * ff), F32),
                     wod, wos, fnw, wup, cw, cb, wdn, finw, ts_p)

    proj_s, s_dk, s_dv, s_sk, s_sv = _inproj(x_sample, anw, w_in_bf, ss)
    od_s, os_s = _sample_attn(lam, proj_s, cache_diff_k, cache_diff_v, cache_sb_k, cache_sb_v,
                              bias_sp, bias_sn, subw, nw2, tri, n_dh)
    y_s, cv_s = _mlp(x_sample, od_s, os_s, state8(state_conv[0]),
                     wod, wos, fnw, wup, cw, cb, wdn, finw, ss)

    tail = lambda cv: cv[None, :, CARRY_ROWS - 2:, :]
    return (y_p, y_s, p_dk, p_dv, p_sk, p_sv, tail(cv_p), s_dk, s_dv, s_sk, s_sv, tail(cv_s))
```

```python
import functools
import math

import numpy as np
import jax
import jax.numpy as jnp
from jax import lax
from jax.experimental import pallas as pl
from jax.experimental.pallas import tpu as pltpu

F32 = jnp.float32
BF16 = jnp.bfloat16

HEAD_DIM = 64
LANES = 128
CHUNK = 64
N_BUCKETS = 32
EPS = 1e-6
NEG = -1e30
LOG2E = 1.4426950408889634
SIGN_BIT = np.int32(-2 ** 31)
ATT_TILE = 256
VMEM_LIMIT = 56 * 1024 * 1024


def _rel_bucket_np(rel):
    half = N_BUCKETS // 2
    max_exact = half // 2
    n = np.abs(rel)
    thresholds = (12, 16, 23, 32, 46, 64, 91)
    large = max_exact + sum((n >= t).astype(np.int64) for t in thresholds)
    return (np.where(rel > 0, half, 0) + np.where(n < max_exact, n, large)).astype(np.int32)


def _bucket_tiles(qpos, kpos):
    rel = kpos[None, :] - qpos[:, None]
    vis = (kpos[None, :] // CHUNK) <= (qpos[:, None] // CHUNK)
    return np.where(vis, _rel_bucket_np(rel), -1).astype(np.int32)


def _nt_dot(a, b):
    return lax.dot_general(a, b, (((1,), (1,)), ((), ())), preferred_element_type=F32)


def _dot(a, b):
    return jnp.dot(a, b, preferred_element_type=F32)


def _rms(x, w):
    return x * lax.rsqrt(jnp.mean(x * x, axis=-1, keepdims=True) + EPS) * w


def _prep_kernel(rb_ref, lq1_ref, lk1_ref, lq2_ref, lk2_ref, bp_ref, bsp_ref, bsn_ref,
                 bias_p_ref, bias_sp_ref, bias_sn_ref, lam_ref, *, lam_init, far_bucket):
    h = pl.program_id(0)
    pairs = ((bp_ref, bias_p_ref), (bsp_ref, bias_sp_ref), (bsn_ref, bias_sn_ref))
    for _, out in pairs:
        out[...] = jnp.full(out.shape, NEG, F32)

    def body(b, carry):
        val = rb_ref[b, h]
        for idx, out in pairs:
            out[0] = jnp.where(idx[...] == b, val, out[0])
        return carry

    lax.fori_loop(0, N_BUCKETS, body, 0)
    bias_p_ref[0] = bias_p_ref[0] - rb_ref[far_bucket, h]
    s1 = jnp.sum(lq1_ref[...] * lk1_ref[...], axis=-1, keepdims=True)
    s2 = jnp.sum(lq2_ref[...] * lk2_ref[...], axis=-1, keepdims=True)
    lam_ref[...] = jnp.exp(s1) - jnp.exp(s2) + lam_init


def _prep(rel_bias, lq1, lk1, lq2, lk2, bkt_p, bkt_sp, bkt_sn, lam_init, far_bucket):
    nh = rel_bias.shape[1]
    full = lambda a: pl.BlockSpec(a.shape, lambda h: (0,) * a.ndim)
    head = lambda a: pl.BlockSpec((1,) + a.shape, lambda h: (h,) + (0,) * a.ndim)
    vec = pl.BlockSpec((1, HEAD_DIM), lambda h: (0, 0))
    return pl.pallas_call(
        functools.partial(_prep_kernel, lam_init=lam_init, far_bucket=far_bucket),
        grid=(nh,),
        in_specs=[pl.BlockSpec(memory_space=pltpu.SMEM), vec, vec, vec, vec,
                  full(bkt_p), full(bkt_sp), full(bkt_sn)],
        out_specs=[head(bkt_p), head(bkt_sp), head(bkt_sn), pl.BlockSpec((1, 1), lambda h: (0, 0))],
        out_shape=[jax.ShapeDtypeStruct((nh,) + bkt_p.shape, F32),
                   jax.ShapeDtypeStruct((nh,) + bkt_sp.shape, F32),
                   jax.ShapeDtypeStruct((nh,) + bkt_sn.shape, F32),
                   jax.ShapeDtypeStruct((1, 1), F32)],
        name="prep",
    )(rel_bias, lq1, lk1, lq2, lk2, bkt_p, bkt_sp, bkt_sn)


def _inproj_kernel(x_ref, nw_ref, w_ref, proj_ref, dk_ref, dv_ref, sk_ref, sv_ref, *, width):
    h = _rms(x_ref[0], nw_ref[...]).astype(BF16)
    n_dh = width // LANES
    n_sh = width // HEAD_DIM
    for g in range(6):
        acc = _dot(h, w_ref[:, g * width:(g + 1) * width])
        proj_ref[0, :, g * width:(g + 1) * width] = acc.astype(BF16)
        if g in (1, 2):
            out = dk_ref if g == 1 else dv_ref
            for hh in range(n_dh):
                out[0, 0, hh] = acc[:, hh * LANES:(hh + 1) * LANES]
        if g in (4, 5):
            out = sk_ref if g == 4 else sv_ref
            for hh in range(n_sh):
                out[0, 0, hh] = acc[:, hh * HEAD_DIM:(hh + 1) * HEAD_DIM]


def _inproj(x, nw, w_in_bf, ts):
    b, s, d = x.shape
    cols = w_in_bf.shape[1]
    width = cols // 6
    n_dh, n_sh = width // LANES, width // HEAD_DIM
    kv = lambda nh, hd: pl.BlockSpec((1, 1, nh, ts, hd), lambda i, j: (0, i, 0, j, 0))
    return pl.pallas_call(
        functools.partial(_inproj_kernel, width=width),
        grid=(b, s // ts),
        in_specs=[pl.BlockSpec((1, ts, d), lambda i, j: (i, j, 0)),
                  pl.BlockSpec((1, d), lambda i, j: (0, 0)),
                  pl.BlockSpec((d, cols), lambda i, j: (0, 0))],
        out_specs=[pl.BlockSpec((1, ts, cols), lambda i, j: (i, j, 0)),
                   kv(n_dh, LANES), kv(n_dh, LANES), kv(n_sh, HEAD_DIM), kv(n_sh, HEAD_DIM)],
        out_shape=[jax.ShapeDtypeStruct((b, s, cols), BF16),
                   jax.ShapeDtypeStruct((1, b, n_dh, s, LANES), F32),
                   jax.ShapeDtypeStruct((1, b, n_dh, s, LANES), F32),
                   jax.ShapeDtypeStruct((1, b, n_sh, s, HEAD_DIM), F32),
                   jax.ShapeDtypeStruct((1, b, n_sh, s, HEAD_DIM), F32)],
        compiler_params=pltpu.CompilerParams(vmem_limit_bytes=VMEM_LIMIT),
        name="inproj",
    )(x, nw, w_in_bf)


def _sb_block(qm, k, v, carry, tri, cmask):
    z = _nt_dot(qm, k)
    sp = jnp.log1p(jnp.exp(-jnp.abs(z)))
    lb = jnp.minimum(z, 0.0) - sp
    l1m = jnp.minimum(-z, 0.0) - sp
    if cmask is not None:
        l1m = jnp.where(cmask, l1m, 0.0)
    hi = l1m.astype(BF16)
    lo = (l1m - hi.astype(F32)).astype(BF16)
    after = _dot(hi, tri) + _dot(lo, tri) + carry
    w = jnp.exp(lb + after)
    if cmask is not None:
        w = jnp.where(cmask, w, 0.0)
    contrib = _dot(w.astype(BF16), v)
    return contrib, carry + jnp.sum(l1m, axis=-1, keepdims=True)


def _diff_finish(a1, l1, a2, l2, lam, subw):
    o = a1 / l1 - lam * (a2 / l2)
    return _rms(o, subw)


def _pair_rms(o, lane, w):
    o2 = o * o
    tot = jnp.sum(o2, axis=-1, keepdims=True)
    lo = jnp.sum(jnp.where(lane < HEAD_DIM, o2, 0.0), axis=-1, keepdims=True)
    ms = jnp.where(lane < HEAD_DIM, lo, tot - lo) * (1.0 / HEAD_DIM)
    return o * lax.rsqrt(ms + EPS) * w


def _diff_kernel(lam_ref, q_ref, k_ref, v_ref, bias_ref, subw_ref, o_ref, *, tile, nq):
    lam = lam_ref[0, 0]
    lane = lax.broadcasted_iota(jnp.int32, (tile, LANES), 1)
    subw = subw_ref[...]
    rowmax = lambda t: jnp.max(t, axis=-1, keepdims=True)
    rowsum = lambda t: jnp.sum(t, axis=-1, keepdims=True)

    for qi in range(nq):
        q = q_ref[0, qi * tile:(qi + 1) * tile, :] * jnp.asarray(HEAD_DIM ** -0.5, BF16)
        kk = k_ref[0, :(qi + 1) * tile, :]
        vv = v_ref[0, :(qi + 1) * tile, :]
        outs = []
        for qm in (jnp.where(lane < HEAD_DIM, q, jnp.zeros_like(q)),
                   jnp.where(lane >= HEAD_DIM, q, jnp.zeros_like(q))):
            s = _nt_dot(qm, kk)
            parts = [s[:, qi * tile:] + bias_ref[0, 0]]
            if qi >= 1:
                parts.append(s[:, (qi - 1) * tile:qi * tile] + bias_ref[0, 1])
            if qi >= 2:
                parts.append(s[:, :(qi - 1) * tile])
            m = functools.reduce(jnp.maximum, [rowmax(t) for t in parts])
            ps = [jnp.exp(t - m) for t in parts]
            l = functools.reduce(jnp.add, [rowsum(t) for t in ps])
            pb = jnp.concatenate([t.astype(BF16) for t in reversed(ps)], axis=1)
            outs.append((_dot(pb, vv), l))
        (a1, l1), (a2, l2) = outs
        o_ref[0, qi * tile:(qi + 1) * tile, :] = _diff_finish(a1, l1, a2, l2, lam, subw).astype(BF16)


def _diff_attn(lam, proj, bias_p, subw, n_heads):
    b, s, _ = proj.shape
    tile = ATT_TILE
    blk = lambda off: pl.BlockSpec((1, s, LANES), lambda i, h: (i, 0, off + h))
    return pl.pallas_call(
        functools.partial(_diff_kernel, tile=tile, nq=s // tile),
        grid=(b, n_heads),
        in_specs=[pl.BlockSpec(memory_space=pltpu.SMEM),
                  blk(0), blk(n_heads), blk(2 * n_heads),
                  pl.BlockSpec((1, 2, tile, tile), lambda i, h: (h, 0, 0, 0)),
                  pl.BlockSpec((1, LANES), lambda i, h: (0, 0))],
        out_specs=pl.BlockSpec((1, s, LANES), lambda i, h: (i, 0, h)),
        out_shape=jax.ShapeDtypeStruct((b, s, n_heads * LANES), BF16),
        compiler_params=pltpu.CompilerParams(vmem_limit_bytes=VMEM_LIMIT),
        name="diff_attn",
    )(lam, proj, proj, proj, bias_p, subw)


def _sb_rows(qm, kk, vv, tri, cmask, tile):
    nb = kk.shape[0] // tile
    z2 = _nt_dot(qm, kk) * LOG2E
    neg_abs = lax.bitcast_convert_type(lax.bitcast_convert_type(z2, jnp.int32) | SIGN_BIT, F32)
    sp = jnp.log2(1.0 + jnp.exp2(neg_abs))
    lb = jnp.minimum(z2, 0.0) - sp
    l1m = lb - z2
    run = None
    w_parts = [None] * nb
    for j in range(nb - 1, -1, -1):
        cols = slice(j * tile, (j + 1) * tile)
        lj = l1m[:, cols]
        if j == nb - 1:
            lj = jnp.where(cmask, lj, 0.0)
        hi = lj.astype(BF16)
        lo = (lj - hi.astype(F32)).astype(BF16)
        x = lb[:, cols] + (_dot(hi, tri) + _dot(lo, tri))
        if run is not None:
            x = x + run
        wj = jnp.exp2(x)
        if j == nb - 1:
            wj = jnp.where(cmask, wj, 0.0)
        w_parts[j] = wj.astype(BF16)
        if j > 0:
            rs = jnp.sum(lj, axis=-1, keepdims=True)
            run = rs if run is None else run + rs
    return _dot(jnp.concatenate(w_parts, axis=1), vv)


def _sb_kernel(q_ref, k_ref, v_ref, tri_ref, nw_ref, o_ref, *, tile, nq):
    lane = lax.broadcasted_iota(jnp.int32, (tile, LANES), 1)
    row = lax.broadcasted_iota(jnp.int32, (tile, tile), 0)
    colv = lax.broadcasted_iota(jnp.int32, (tile, tile), 1)
    cmask = colv < row
    tri = tri_ref[...]
    nw = nw_ref[...]

    for qi in range(nq):
        q = q_ref[0, qi * tile:(qi + 1) * tile, :] * jnp.asarray(HEAD_DIM ** -0.5, BF16)
        kk = k_ref[0, :(qi + 1) * tile, :]
        vv = v_ref[0, :(qi + 1) * tile, :]
        halves = []
        for hh in range(2):
            in_half = (lane >= hh * HEAD_DIM) & (lane < (hh + 1) * HEAD_DIM)
            qm = jnp.where(in_half, q, jnp.zeros_like(q))
            halves.append(_sb_rows(qm, kk, vv, tri, cmask, tile))
        o = jnp.where(lane < HEAD_DIM, halves[0], halves[1])
        o_ref[0, qi * tile:(qi + 1) * tile, :] = _pair_rms(o, lane, nw).astype(BF16)


def _sb_attn(proj, tri, nw2, n_pairs, off):
    b, s, _ = proj.shape
    tile = ATT_TILE
    blk = lambda o: pl.BlockSpec((1, s, LANES), lambda i, p: (i, 0, o + p))
    return pl.pallas_call(
        functools.partial(_sb_kernel, tile=tile, nq=s // tile),
        grid=(b, n_pairs),
        in_specs=[blk(off), blk(off + n_pairs), blk(off + 2 * n_pairs),
                  pl.BlockSpec((tile, tile), lambda i, p: (0, 0)),
                  pl.BlockSpec((1, LANES), lambda i, p: (0, 0))],
        out_specs=pl.BlockSpec((1, s, LANES), lambda i, p: (i, 0, p)),
        out_shape=jax.ShapeDtypeStruct((b, s, n_pairs * LANES), BF16),
        compiler_params=pltpu.CompilerParams(vmem_limit_bytes=VMEM_LIMIT),
        name="sb_attn",
    )(proj, proj, proj, tri, nw2)


def _sample_attn_kernel(lam_ref, qd_ref, kd_ref, vd_ref, qs_ref, ks_ref, vs_ref,
                        cdk_ref, cdv_ref, csk_ref, csv_ref, bsp_ref, bsn_ref,
                        subw_ref, nw_ref, tri_ref, od_ref, os_ref, *, sq, past, tile):
    lam = lam_ref[0, 0]
    scale = jnp.asarray(HEAD_DIM ** -0.5, BF16)
    lane = lax.broadcasted_iota(jnp.int32, (sq, LANES), 1)

    q = qd_ref[0] * scale
    kc = cdk_ref[0, 0, 0].astype(BF16)
    vc = cdv_ref[0, 0, 0].astype(BF16)
    kn = kd_ref[0]
    vn = vd_ref[0]
    bias_c = bsp_ref[0]
    bias_n = bsn_ref[0]
    outs = []
    for qm in (jnp.where(lane < HEAD_DIM, q, jnp.zeros_like(q)),
               jnp.where(lane >= HEAD_DIM, q, jnp.zeros_like(q))):
        sc = _nt_dot(qm, kc) + bias_c
        sn = _nt_dot(qm, kn) + bias_n
        m = jnp.maximum(jnp.max(sc, axis=-1, keepdims=True), jnp.max(sn, axis=-1, keepdims=True))
        pc = jnp.exp(sc - m)
        pn = jnp.exp(sn - m)
        l = jnp.sum(pc, axis=-1, keepdims=True) + jnp.sum(pn, axis=-1, keepdims=True)
        outs.append((_dot(pc.astype(BF16), vc) + _dot(pn.astype(BF16), vn), l))
    (a1, l1), (a2, l2) = outs
    od_ref[0] = _diff_finish(a1, l1, a2, l2, lam, subw_ref[...]).astype(BF16)

    row = lax.broadcasted_iota(jnp.int32, (sq, sq), 0)
    colv = lax.broadcasted_iota(jnp.int32, (sq, sq), 1)
    cmask = colv < row
    tri = tri_ref[...]
    qs = qs_ref[0] * scale
    halves = []
    for hh in range(2):
        sl = slice(hh * HEAD_DIM, (hh + 1) * HEAD_DIM)
        qh = qs[:, sl]
        acc, run = _sb_block(qh, ks_ref[0][:, sl], vs_ref[0][:, sl],
                             jnp.zeros((sq, 1), F32), tri[:sq, :sq], cmask)
        for kb in range(past // tile - 1, -1, -1):
            kblk = csk_ref[0, 0, hh, kb * tile:(kb + 1) * tile, :].astype(BF16)
            vblk = csv_ref[0, 0, hh, kb * tile:(kb + 1) * tile, :].astype(BF16)
            contrib, run = _sb_block(qh, kblk, vblk, run, tri, None)
            acc = acc + contrib
        halves.append(acc)
    o = jnp.concatenate(halves, axis=-1)
    os_ref[0] = _pair_rms(o, lane, nw_ref[...]).astype(BF16)


def _sample_attn(lam, proj, cdk, cdv, csk, csv, bias_sp, bias_sn, subw, nw2, tri, n_dh):
    b, sq, _ = proj.shape
    past = cdk.shape[3]
    tile = ATT_TILE
    blk = lambda off: pl.BlockSpec((1, sq, LANES), lambda i, h: (i, 0, off + h))
    cd = pl.BlockSpec((1, 1, 1, past, LANES), lambda i, h: (0, i, h, 0, 0))
    cs = pl.BlockSpec((1, 1, 2, past, HEAD_DIM), lambda i, h: (0, i, h, 0, 0))
    const2 = lambda a: pl.BlockSpec(a.shape, lambda i, h: (0, 0))
    out = pl.BlockSpec((1, sq, LANES), lambda i, h: (i, 0, h))
    return pl.pallas_call(
        functools.partial(_sample_attn_kernel, sq=sq, past=past, tile=tile),
        grid=(b, n_dh),
        in_specs=[pl.BlockSpec(memory_space=pltpu.SMEM),
                  blk(0), blk(n_dh), blk(2 * n_dh), blk(3 * n_dh), blk(4 * n_dh), blk(5 * n_dh),
                  cd, cd, cs, cs,
                  pl.BlockSpec((1, sq, past), lambda i, h: (h, 0, 0)),
                  pl.BlockSpec((1, sq, sq), lambda i, h: (h, 0, 0)),
                  const2(subw), const2(nw2), const2(tri)],
        out_specs=[out, out],
        out_shape=[jax.ShapeDtypeStruct((b, sq, n_dh * LANES), BF16)] * 2,
        compiler_params=pltpu.CompilerParams(vmem_limit_bytes=VMEM_LIMIT),
        name="sample_attn",
    )(lam, proj, proj, proj, proj, proj, proj, cdk, cdv, csk, csv, bias_sp, bias_sn, subw, nw2, tri)


CARRY_ROWS = 8


def _mlp_kernel(x_ref, od_ref, os_ref, cinit_ref, wod_ref, wos_ref, fnw_ref, wup_ref, cw_ref, cb_ref,
                wdn_ref, finw_ref, y_ref, nconv_ref, carry_ref, *, ts, ff, fc):
    @pl.when(pl.program_id(1) == 0)
    def _():
        carry_ref[...] = cinit_ref[0]

    x1 = x_ref[0] + _dot(od_ref[0], wod_ref[...]) + _dot(os_ref[0], wos_ref[...])
    h2 = _rms(x1, fnw_ref[...]).astype(BF16)
    row = lax.broadcasted_iota(jnp.int32, (ts, fc), 0)

    def conv(c0):
        cols = slice(c0, c0 + fc)
        u = _dot(h2, wup_ref[:, cols])
        prev0 = carry_ref[CARRY_ROWS - 2:CARRY_ROWS - 1, cols]
        prev1 = carry_ref[CARRY_ROWS - 1:CARRY_ROWS, cols]
        u1 = jnp.where(row == 0, prev1, pltpu.roll(u, 1, 0))
        u2 = jnp.where(row == 0, prev0, jnp.where(row == 1, prev1, pltpu.roll(u, 2, 0)))
        out = cb_ref[:, cols] + u2 * cw_ref[0:1, cols] + u1 * cw_ref[1:2, cols] + u * cw_ref[2:3, cols]
        tail = u[ts - CARRY_ROWS:, :]
        carry_ref[:, cols] = tail
        nconv_ref[0, :, cols] = tail
        return out

    acc = jnp.zeros_like(x1)
    for c in range(ff // fc):
        g = conv(c * fc)
        v = conv(ff + c * fc)
        act = (g / (1.0 + jnp.exp(-g)) * v).astype(BF16)
        acc = acc + _dot(act, wdn_ref[c * fc:(c + 1) * fc, :])
    y_ref[0] = _rms(x1 + acc, finw_ref[...])


def _mlp(x, od, osb, cinit, wod, wos, fnw, wup, cw, cb, wdn, finw, ts):
    b, s, d = x.shape
    ff = wdn.shape[0]
    half = od.shape[2]
    fc = 256
    const = lambda a: pl.BlockSpec(a.shape, lambda i, j: (0, 0))
    tok = lambda w: pl.BlockSpec((1, ts, w), lambda i, j: (i, j, 0))
    state = pl.BlockSpec((1, CARRY_ROWS, 2 * ff), lambda i, j: (i, 0, 0))
    return pl.pallas_call(
        functools.partial(_mlp_kernel, ts=ts, ff=ff, fc=fc),
        grid=(b, s // ts),
        in_specs=[tok(d), tok(half), tok(half), state, const(wod), const(wos), const(fnw), const(wup),
                  const(cw), const(cb), const(wdn), const(finw)],
        out_specs=[tok(d), state],
        out_shape=[jax.ShapeDtypeStruct((b, s, d), F32),
                   jax.ShapeDtypeStruct((b, CARRY_ROWS, 2 * ff), F32)],
        scratch_shapes=[pltpu.VMEM((CARRY_ROWS, 2 * ff), F32)],
        compiler_params=pltpu.CompilerParams(dimension_semantics=("arbitrary", "arbitrary"),
                                             vmem_limit_bytes=VMEM_LIMIT),
        name="mlp",
    )(x, od, osb, cinit, wod, wos, fnw, wup, cw, cb, wdn, finw)


def kernel(x_prompt, x_sample, cache_diff_k, cache_diff_v, cache_sb_k, cache_sb_v, state_conv, attn_norm_w, w_in, lambda_q1, lambda_k1, lambda_q2, lambda_k2, diff_subln_w, sb_norm_w, w_out, ffn_norm_w, w_up, conv_w, conv_b, w_down, rel_bias, final_norm_w):
    depth = w_in.shape[0]
    assert depth == 1, "single-layer step"
    bp, sp, d = x_prompt.shape
    bs, ss, _ = x_sample.shape
    past = cache_diff_k.shape[3]
    n_dh = cache_diff_k.shape[2]
    n_sh = cache_sb_k.shape[2]
    ff = w_down.shape[1]
    tile = ATT_TILE
    assert sp % tile == 0 and past % tile == 0 and tile % CHUNK == 0 and n_sh == 2 * n_dh
    assert ss % CARRY_ROWS == 0
    lam_init = 0.8 - 0.6 * math.exp(-0.3 * 0)

    loc = np.arange(tile)
    bkt_p = np.stack([_bucket_tiles(loc + tile, loc + tile), _bucket_tiles(loc + tile, loc)])
    far_bucket = int(_rel_bucket_np(np.array([-(tile + 1)]))[0])
    assert far_bucket == int(_rel_bucket_np(np.array([-(sp + past)]))[0])
    qpos_s = past + np.arange(ss)
    bkt_sp = _bucket_tiles(qpos_s, np.arange(past))
    bkt_sn = _bucket_tiles(qpos_s, qpos_s)

    row = lambda v: v.reshape(1, -1).astype(F32)
    bias_p, bias_sp, bias_sn, lam = _prep(
        rel_bias.astype(F32), row(lambda_q1[0]), row(lambda_k1[0]), row(lambda_q2[0]), row(lambda_k2[0]),
        jnp.asarray(bkt_p), jnp.asarray(bkt_sp), jnp.asarray(bkt_sn), lam_init, far_bucket)

    w_in_bf = w_in[0].astype(BF16)
    w_out_bf = w_out[0].astype(BF16)
    half = n_dh * LANES
    wod, wos = w_out_bf[:half], w_out_bf[half:]
    wup = w_up[0].astype(BF16)
    wdn = w_down[0].astype(BF16)
    anw, fnw, finw = row(attn_norm_w[0]), row(ffn_norm_w[0]), row(final_norm_w)
    subw = row(diff_subln_w[0]) * (1.0 - lam_init)
    nw2 = jnp.tile(row(sb_norm_w[0]), (1, 2))
    cw, cb = conv_w[0].astype(F32), row(conv_b[0])
    tri = jnp.asarray(np.triu(np.ones((tile, tile), np.float32), 1).T, BF16)

    def state8(st):
        return jnp.pad(st, ((0, 0), (CARRY_ROWS - st.shape[1], 0), (0, 0)))

    ts_p = min(512, sp)
    proj_p, p_dk, p_dv, p_sk, p_sv = _inproj(x_prompt, anw, w_in_bf, ts_p)
    od_p = _diff_attn(lam, proj_p, bias_p, subw, n_dh)
    os_p = _sb_attn(proj_p, tri, nw2, n_dh, 3 * n_dh)
    y_p, cv_p = _mlp(x_prompt, od_p, os_p, jnp.zeros((bp, CARRY_ROWS, 2 * ff), F32),
                     wod, wos, fnw, wup, cw, cb, wdn, finw, ts_p)

    proj_s, s_dk, s_dv, s_sk, s_sv = _inproj(x_sample, anw, w_in_bf, ss)
    od_s, os_s = _sample_attn(lam, proj_s, cache_diff_k, cache_diff_v, cache_sb_k, cache_sb_v,
                              bias_sp, bias_sn, subw, nw2, tri, n_dh)
    y_s, cv_s = _mlp(x_sample, od_s, os_s, state8(state_conv[0]),
                     wod, wos, fnw, wup, cw, cb, wdn, finw, ss)

    tail = lambda cv: cv[None, :, CARRY_ROWS - 2:, :]
    return (y_p, y_s, p_dk, p_dv, p_sk, p_sv, tail(cv_p), s_dk, s_dv, s_sk, s_sv, tail(cv_s))
```

```python
import functools
import math

import numpy as np
import jax
import jax.numpy as jnp
from jax import lax
from jax.experimental import pallas as pl
from jax.experimental.pallas import tpu as pltpu

F32 = jnp.float32
BF16 = jnp.bfloat16

HEAD_DIM = 64
LANES = 128
CHUNK = 64
N_BUCKETS = 32
EPS = 1e-6
NEG = -1e30
LOG2E = 1.4426950408889634
SIGN_BIT = np.int32(-2 ** 31)
ATT_TILE = 256
VMEM_LIMIT = 56 * 1024 * 1024


def _rel_bucket_np(rel):
    half = N_BUCKETS // 2
    max_exact = half // 2
    n = np.abs(rel)
    thresholds = (12, 16, 23, 32, 46, 64, 91)
    large = max_exact + sum((n >= t).astype(np.int64) for t in thresholds)
    return (np.where(rel > 0, half, 0) + np.where(n < max_exact, n, large)).astype(np.int32)


def _bucket_tiles(qpos, kpos):
    rel = kpos[None, :] - qpos[:, None]
    vis = (kpos[None, :] // CHUNK) <= (qpos[:, None] // CHUNK)
    return np.where(vis, _rel_bucket_np(rel), -1).astype(np.int32)


def _nt_dot(a, b):
    return lax.dot_general(a, b, (((1,), (1,)), ((), ())), preferred_element_type=F32)


def _dot(a, b):
    return jnp.dot(a, b, preferred_element_type=F32)


def _rms(x, w):
    return x * lax.rsqrt(jnp.mean(x * x, axis=-1, keepdims=True) + EPS) * w


def _prep_kernel(rb_ref, lq1_ref, lk1_ref, lq2_ref, lk2_ref, bp_ref, bsp_ref, bsn_ref,
                 bias_p_ref, bias_sp_ref, bias_sn_ref, lam_ref, *, lam_init, far_bucket):
    h = pl.program_id(0)
    pairs = ((bp_ref, bias_p_ref), (bsp_ref, bias_sp_ref), (bsn_ref, bias_sn_ref))
    for _, out in pairs:
        out[...] = jnp.full(out.shape, NEG, F32)

    def body(b, carry):
        val = rb_ref[b, h]
        for idx, out in pairs:
            out[0] = jnp.where(idx[...] == b, val, out[0])
        return carry

    lax.fori_loop(0, N_BUCKETS, body, 0)
    bias_p_ref[0] = bias_p_ref[0] - rb_ref[far_bucket, h]
    s1 = jnp.sum(lq1_ref[...] * lk1_ref[...], axis=-1, keepdims=True)
    s2 = jnp.sum(lq2_ref[...] * lk2_ref[...], axis=-1, keepdims=True)
    lam_ref[...] = jnp.exp(s1) - jnp.exp(s2) + lam_init


def _prep(rel_bias, lq1, lk1, lq2, lk2, bkt_p, bkt_sp, bkt_sn, lam_init, far_bucket):
    nh = rel_bias.shape[1]
    full = lambda a: pl.BlockSpec(a.shape, lambda h: (0,) * a.ndim)
    head = lambda a: pl.BlockSpec((1,) + a.shape, lambda h: (h,) + (0,) * a.ndim)
    vec = pl.BlockSpec((1, HEAD_DIM), lambda h: (0, 0))
    return pl.pallas_call(
        functools.partial(_prep_kernel, lam_init=lam_init, far_bucket=far_bucket),
        grid=(nh,),
        in_specs=[pl.BlockSpec(memory_space=pltpu.SMEM), vec, vec, vec, vec,
                  full(bkt_p), full(bkt_sp), full(bkt_sn)],
        out_specs=[head(bkt_p), head(bkt_sp), head(bkt_sn), pl.BlockSpec((1, 1), lambda h: (0, 0))],
        out_shape=[jax.ShapeDtypeStruct((nh,) + bkt_p.shape, F32),
                   jax.ShapeDtypeStruct((nh,) + bkt_sp.shape, F32),
                   jax.ShapeDtypeStruct((nh,) + bkt_sn.shape, F32),
                   jax.ShapeDtypeStruct((1, 1), F32)],
        name="prep",
    )(rel_bias, lq1, lk1, lq2, lk2, bkt_p, bkt_sp, bkt_sn)


def _inproj_kernel(x_ref, nw_ref, w_ref, proj_ref, dk_ref, dv_ref, sk_ref, sv_ref, *, width):
    h = _rms(x_ref[0], nw_ref[...]).astype(BF16)
    n_dh = width // LANES
    n_sh = width // HEAD_DIM
    for g in range(6):
        acc = _dot(h, w_ref[:, g * width:(g + 1) * width])
        proj_ref[0, :, g * width:(g + 1) * width] = acc.astype(BF16)
        if g in (1, 2):
            out = dk_ref if g == 1 else dv_ref
            for hh in range(n_dh):
                out[0, 0, hh] = acc[:, hh * LANES:(hh + 1) * LANES]
        if g in (4, 5):
            out = sk_ref if g == 4 else sv_ref
            for hh in range(n_sh):
                out[0, 0, hh] = acc[:, hh * HEAD_DIM:(hh + 1) * HEAD_DIM]


def _inproj(x, nw, w_in_bf, ts):
    b, s, d = x.shape
    cols = w_in_bf.shape[1]
    width = cols // 6
    n_dh, n_sh = width // LANES, width // HEAD_DIM
    kv = lambda nh, hd: pl.BlockSpec((1, 1, nh, ts, hd), lambda i, j: (0, i, 0, j, 0))
    return pl.pallas_call(
        functools.partial(_inproj_kernel, width=width),
        grid=(b, s // ts),
        in_specs=[pl.BlockSpec((1, ts, d), lambda i, j: (i, j, 0)),
                  pl.BlockSpec((1, d), lambda i, j: (0, 0)),
                  pl.BlockSpec((d, cols), lambda i, j: (0, 0))],
        out_specs=[pl.BlockSpec((1, ts, cols), lambda i, j: (i, j, 0)),
                   kv(n_dh, LANES), kv(n_dh, LANES), kv(n_sh, HEAD_DIM), kv(n_sh, HEAD_DIM)],
        out_shape=[jax.ShapeDtypeStruct((b, s, cols), BF16),
                   jax.ShapeDtypeStruct((1, b, n_dh, s, LANES), F32),
                   jax.ShapeDtypeStruct((1, b, n_dh, s, LANES), F32),
                   jax.ShapeDtypeStruct((1, b, n_sh, s, HEAD_DIM), F32),
                   jax.ShapeDtypeStruct((1, b, n_sh, s, HEAD_DIM), F32)],
        compiler_params=pltpu.CompilerParams(vmem_limit_bytes=VMEM_LIMIT),
        name="inproj",
    )(x, nw, w_in_bf)


def _diff_finish(a1, l1, a2, l2, lam, subw):
    o = a1 / l1 - lam * (a2 / l2)
    return _rms(o, subw)


def _pair_rms(o, lane, w):
    o2 = o * o
    tot = jnp.sum(o2, axis=-1, keepdims=True)
    lo = jnp.sum(jnp.where(lane < HEAD_DIM, o2, 0.0), axis=-1, keepdims=True)
    ms = jnp.where(lane < HEAD_DIM, lo, tot - lo) * (1.0 / HEAD_DIM)
    return o * lax.rsqrt(ms + EPS) * w


def _diff_kernel(lam_ref, q_ref, k_ref, v_ref, bias_ref, subw_ref, o_ref, *, tile, nq):
    lam = lam_ref[0, 0]
    lane = lax.broadcasted_iota(jnp.int32, (tile, LANES), 1)
    subw = subw_ref[...]
    rowmax = lambda t: jnp.max(t, axis=-1, keepdims=True)
    rowsum = lambda t: jnp.sum(t, axis=-1, keepdims=True)

    def scores(qi, half):
        q = q_ref[0, qi * tile:(qi + 1) * tile, :] * jnp.asarray(HEAD_DIM ** -0.5, BF16)
        in_half = (lane >= half * HEAD_DIM) & (lane < (half + 1) * HEAD_DIM)
        return _nt_dot(jnp.where(in_half, q, jnp.zeros_like(q)), k_ref[0, :(qi + 1) * tile, :])

    def attend(qi, s):
        parts = [s[:, qi * tile:] + bias_ref[0, 0]]
        if qi >= 1:
            parts.append(s[:, (qi - 1) * tile:qi * tile] + bias_ref[0, 1])
        if qi >= 2:
            parts.append(s[:, :(qi - 1) * tile])
        m = functools.reduce(jnp.maximum, [rowmax(t) for t in parts])
        ps = [jnp.exp(t - m) for t in parts]
        l = functools.reduce(jnp.add, [rowsum(t) for t in ps])
        pb = jnp.concatenate([t.astype(BF16) for t in reversed(ps)], axis=1)
        return _dot(pb, v_ref[0, :(qi + 1) * tile, :]), l

    rows = [(qi, half) for qi in range(nq) for half in range(2)]
    pending, done = {}, {}
    for t in range(len(rows) + 1):
        if t < len(rows):
            pending[rows[t]] = scores(*rows[t])
        if t >= 1:
            qi, half = rows[t - 1]
            done[half] = attend(qi, pending.pop((qi, half)))
            if half == 1:
                (a1, l1), (a2, l2) = done[0], done[1]
                o_ref[0, qi * tile:(qi + 1) * tile, :] = _diff_finish(a1, l1, a2, l2, lam, subw).astype(BF16)


def _diff_attn(lam, proj, bias_p, subw, n_heads):
    b, s, _ = proj.shape
    tile = ATT_TILE
    blk = lambda off: pl.BlockSpec((1, s, LANES), lambda i, h: (i, 0, off + h))
    return pl.pallas_call(
        functools.partial(_diff_kernel, tile=tile, nq=s // tile),
        grid=(b, n_heads),
        in_specs=[pl.BlockSpec(memory_space=pltpu.SMEM),
                  blk(0), blk(n_heads), blk(2 * n_heads),
                  pl.BlockSpec((1, 2, tile, tile), lambda i, h: (h, 0, 0, 0)),
                  pl.BlockSpec((1, LANES), lambda i, h: (0, 0))],
        out_specs=pl.BlockSpec((1, s, LANES), lambda i, h: (i, 0, h)),
        out_shape=jax.ShapeDtypeStruct((b, s, n_heads * LANES), BF16),
        compiler_params=pltpu.CompilerParams(vmem_limit_bytes=VMEM_LIMIT),
        name="diff_attn",
    )(lam, proj, proj, proj, bias_p, subw)


def _sb_logits(z2, tri2, tile, cmask=None, run=None, total=False):
    nb = z2.shape[1] // tile
    neg_abs = lax.bitcast_convert_type(lax.bitcast_convert_type(z2, jnp.int32) | SIGN_BIT, F32)
    sp = jnp.log2(1.0 + jnp.exp2(neg_abs))
    lb = jnp.minimum(z2, 0.0) - sp
    l1m = lb - z2
    xs = [None] * nb
    for j in range(nb - 1, -1, -1):
        cols = slice(j * tile, (j + 1) * tile)
        lj = l1m[:, cols]
        if cmask is not None and j == nb - 1:
            lj = jnp.where(cmask, lj, 0.0)
        hi = lj.astype(BF16)
        lo = (lj - hi.astype(F32)).astype(BF16)
        if tile % LANES == 0:
            suffix = _dot(jnp.concatenate([hi, lo], axis=1), tri2)
        else:
            suffix = _dot(hi, tri2) + _dot(lo, tri2)
        x = lb[:, cols] + suffix
        xs[j] = x if run is None else x + run
        if j > 0 or total:
            rs = jnp.sum(lj, axis=-1, keepdims=True)
            run = rs if run is None else run + rs
    return xs, run


def _sb_weights(xs, cmask=None):
    ws = [jnp.exp2(x) for x in xs]
    if cmask is not None:
        ws[-1] = jnp.where(cmask, ws[-1], 0.0)
    ws = [w.astype(BF16) for w in ws]
    return jnp.concatenate(ws, axis=1) if len(ws) > 1 else ws[0]


def _sb_kernel(q_ref, k_ref, v_ref, tri_ref, nw_ref, o_ref, *, tile, nq):
    lane = lax.broadcasted_iota(jnp.int32, (tile, LANES), 1)
    row = lax.broadcasted_iota(jnp.int32, (tile, tile), 0)
    colv = lax.broadcasted_iota(jnp.int32, (tile, tile), 1)
    cmask = colv < row
    tri = tri_ref[...]
    nw = nw_ref[...]

    def scores(qi, half):
        q = q_ref[0, qi * tile:(qi + 1) * tile, :] * jnp.asarray(HEAD_DIM ** -0.5, BF16)
        in_half = (lane >= half * HEAD_DIM) & (lane < (half + 1) * HEAD_DIM)
        return _nt_dot(jnp.where(in_half, q, jnp.zeros_like(q)), k_ref[0, :(qi + 1) * tile, :]) * LOG2E

    rows = [(qi, half) for qi in range(nq) for half in range(2)]
    z2s, xss, halves = {}, {}, {}
    for t in range(len(rows) + 2):
        if t < len(rows):
            z2s[rows[t]] = scores(*rows[t])
        if 1 <= t <= len(rows):
            xss[rows[t - 1]], _ = _sb_logits(z2s.pop(rows[t - 1]), tri, tile, cmask)
        if t >= 2:
            qi, half = rows[t - 2]
            halves[half] = _dot(_sb_weights(xss.pop((qi, half)), cmask), v_ref[0, :(qi + 1) * tile, :])
            if half == 1:
                o = jnp.where(lane < HEAD_DIM, halves[0], halves[1])
                o_ref[0, qi * tile:(qi + 1) * tile, :] = _pair_rms(o, lane, nw).astype(BF16)


def _sb_attn(proj, tri, nw2, n_pairs, off):
    b, s, _ = proj.shape
    tile = ATT_TILE
    blk = lambda o: pl.BlockSpec((1, s, LANES), lambda i, p: (i, 0, o + p))
    return pl.pallas_call(
        functools.partial(_sb_kernel, tile=tile, nq=s // tile),
        grid=(b, n_pairs),
        in_specs=[blk(off), blk(off + n_pairs), blk(off + 2 * n_pairs),
                  pl.BlockSpec(tri.shape, lambda i, p: (0, 0)),
                  pl.BlockSpec((1, LANES), lambda i, p: (0, 0))],
        out_specs=pl.BlockSpec((1, s, LANES), lambda i, p: (i, 0, p)),
        out_shape=jax.ShapeDtypeStruct((b, s, n_pairs * LANES), BF16),
        compiler_params=pltpu.CompilerParams(vmem_limit_bytes=VMEM_LIMIT),
        name="sb_attn",
    )(proj, proj, proj, tri, nw2)


def _sample_attn_kernel(lam_ref, qd_ref, kd_ref, vd_ref, qs_ref, ks_ref, vs_ref,
                        cdk_ref, cdv_ref, csk_ref, csv_ref, bsp_ref, bsn_ref,
                        subw_ref, nw_ref, tri_ref, od_ref, os_ref, *, sq, past, tile):
    lam = lam_ref[0, 0]
    scale = jnp.asarray(HEAD_DIM ** -0.5, BF16)
    lane = lax.broadcasted_iota(jnp.int32, (sq, LANES), 1)

    q = qd_ref[0] * scale
    kc = cdk_ref[0, 0, 0].astype(BF16)
    vc = cdv_ref[0, 0, 0].astype(BF16)
    kn = kd_ref[0]
    vn = vd_ref[0]
    bias_c = bsp_ref[0]
    bias_n = bsn_ref[0]
    outs = []
    for qm in (jnp.where(lane < HEAD_DIM, q, jnp.zeros_like(q)),
               jnp.where(lane >= HEAD_DIM, q, jnp.zeros_like(q))):
        sc = _nt_dot(qm, kc) + bias_c
        sn = _nt_dot(qm, kn) + bias_n
        m = jnp.maximum(jnp.max(sc, axis=-1, keepdims=True), jnp.max(sn, axis=-1, keepdims=True))
        pc = jnp.exp(sc - m)
        pn = jnp.exp(sn - m)
        l = jnp.sum(pc, axis=-1, keepdims=True) + jnp.sum(pn, axis=-1, keepdims=True)
        outs.append((_dot(pc.astype(BF16), vc) + _dot(pn.astype(BF16), vn), l))
    (a1, l1), (a2, l2) = outs
    od_ref[0] = _diff_finish(a1, l1, a2, l2, lam, subw_ref[...]).astype(BF16)

    row = lax.broadcasted_iota(jnp.int32, (sq, sq), 0)
    colv = lax.broadcasted_iota(jnp.int32, (sq, sq), 1)
    cmask = colv < row
    tri2 = tri_ref[...]
    tri_new = tri2[:sq, :sq]
    qs = qs_ref[0] * scale
    heads = [slice(hh * HEAD_DIM, (hh + 1) * HEAD_DIM) for hh in range(2)]
    z_new = [_nt_dot(qs[:, sl], ks_ref[0][:, sl]) * LOG2E for sl in heads]
    z_old = [_dot(qs[:, sl], csk_ref[0, 0, hh].astype(BF16)) * LOG2E for hh, sl in enumerate(heads)]
    halves = []
    for hh, sl in enumerate(heads):
        x_new, run = _sb_logits(z_new[hh], tri_new, sq, cmask, None, total=True)
        x_old, _ = _sb_logits(z_old[hh], tri2, tile, None, run)
        halves.append(_dot(_sb_weights(x_new, cmask), vs_ref[0][:, sl])
                      + _nt_dot(_sb_weights(x_old), csv_ref[0, 0, hh].astype(BF16)))
    o = jnp.concatenate(halves, axis=-1)
    os_ref[0] = _pair_rms(o, lane, nw_ref[...]).astype(BF16)


def _sample_attn(lam, proj, cdk, cdv, csk, csv, bias_sp, bias_sn, subw, nw2, tri, n_dh):
    b, sq, _ = proj.shape
    past = cdk.shape[3]
    tile = ATT_TILE
    blk = lambda off: pl.BlockSpec((1, sq, LANES), lambda i, h: (i, 0, off + h))
    cd = pl.BlockSpec((1, 1, 1, past, LANES), lambda i, h: (0, i, h, 0, 0))
    cs = pl.BlockSpec((1, 1, 2, HEAD_DIM, past), lambda i, h: (0, i, h, 0, 0))
    const2 = lambda a: pl.BlockSpec(a.shape, lambda i, h: (0, 0))
    out = pl.BlockSpec((1, sq, LANES), lambda i, h: (i, 0, h))
    return pl.pallas_call(
        functools.partial(_sample_attn_kernel, sq=sq, past=past, tile=tile),
        grid=(b, n_dh),
        in_specs=[pl.BlockSpec(memory_space=pltpu.SMEM),
                  blk(0), blk(n_dh), blk(2 * n_dh), blk(3 * n_dh), blk(4 * n_dh), blk(5 * n_dh),
                  cd, cd, cs, cs,
                  pl.BlockSpec((1, sq, past), lambda i, h: (h, 0, 0)),
                  pl.BlockSpec((1, sq, sq), lambda i, h: (h, 0, 0)),
                  const2(subw), const2(nw2), const2(tri)],
        out_specs=[out, out],
        out_shape=[jax.ShapeDtypeStruct((b, sq, n_dh * LANES), BF16)] * 2,
        compiler_params=pltpu.CompilerParams(vmem_limit_bytes=VMEM_LIMIT),
        name="sample_attn",
    )(lam, proj, proj, proj, proj, proj, proj, cdk, cdv, csk, csv, bias_sp, bias_sn, subw, nw2, tri)


CARRY_ROWS = 8


def _mlp_kernel(x_ref, od_ref, os_ref, cinit_ref, wod_ref, wos_ref, fnw_ref, wup_ref, cw_ref, cb_ref,
                wdn_ref, finw_ref, y_ref, nconv_ref, carry_ref, *, ts, ff, fc):
    @pl.when(pl.program_id(1) == 0)
    def _():
        carry_ref[...] = cinit_ref[0]

    x1 = x_ref[0] + _dot(od_ref[0], wod_ref[...]) + _dot(os_ref[0], wos_ref[...])
    h2 = _rms(x1, fnw_ref[...]).astype(BF16)
    groups = ts // CARRY_ROWS
    sub = lax.broadcasted_iota(jnp.int32, (groups, CARRY_ROWS, fc), 1)
    nc = ff // fc

    def up(c):
        return tuple(_dot(h2, wup_ref[:, c0:c0 + fc]) for c0 in (c * fc, ff + c * fc))

    def conv(u, c0):
        cols = slice(c0, c0 + fc)
        u3 = u.reshape(groups, CARRY_ROWS, fc)
        before = carry_ref[:, cols]

        def shifted(k):
            rot = pltpu.roll(u3, k, 1)
            above = pltpu.roll(before, k, 0)[None]
            if groups > 1:
                above = jnp.concatenate([above, rot[:-1]], axis=0)
            return jnp.where(sub < k, above, rot).reshape(ts, fc)

        out = (cb_ref[:, cols] + shifted(2) * cw_ref[0:1, cols] + shifted(1) * cw_ref[1:2, cols]
               + u * cw_ref[2:3, cols])
        tail = u[ts - CARRY_ROWS:, :]
        carry_ref[:, cols] = tail
        nconv_ref[0, :, cols] = tail
        return out

    acc = jnp.zeros_like(x1)
    u = up(0)
    acts = []
    for c in range(nc):
        u_next = up(c + 1) if c + 1 < nc else None
        g = conv(u[0], c * fc)
        v = conv(u[1], ff + c * fc)
        acts.append((g / (1.0 + jnp.exp(-g)) * v).astype(BF16))
        if len(acts) == 2 or c == nc - 1:
            act = jnp.concatenate(acts, axis=1) if len(acts) > 1 else acts[0]
            acc = acc + _dot(act, wdn_ref[(c + 1 - len(acts)) * fc:(c + 1) * fc, :])
            acts = []
        u = u_next
    y_ref[0] = _rms(x1 + acc, finw_ref[...])


def _mlp(x, od, osb, cinit, wod, wos, fnw, wup, cw, cb, wdn, finw, ts):
    b, s, d = x.shape
    ff = wdn.shape[0]
    half = od.shape[2]
    fc = 256
    const = lambda a: pl.BlockSpec(a.shape, lambda i, j: (0, 0))
    tok = lambda w: pl.BlockSpec((1, ts, w), lambda i, j: (i, j, 0))
    state = pl.BlockSpec((1, CARRY_ROWS, 2 * ff), lambda i, j: (i, 0, 0))
    return pl.pallas_call(
        functools.partial(_mlp_kernel, ts=ts, ff=ff, fc=fc),
        grid=(b, s // ts),
        in_specs=[tok(d), tok(half), tok(half), state, const(wod), const(wos), const(fnw), const(wup),
                  const(cw), const(cb), const(wdn), const(finw)],
        out_specs=[tok(d), state],
        out_shape=[jax.ShapeDtypeStruct((b, s, d), F32),
                   jax.ShapeDtypeStruct((b, CARRY_ROWS, 2 * ff), F32)],
        scratch_shapes=[pltpu.VMEM((CARRY_ROWS, 2 * ff), F32)],
        compiler_params=pltpu.CompilerParams(dimension_semantics=("arbitrary", "arbitrary"),
                                             vmem_limit_bytes=VMEM_LIMIT),
        name="mlp",
    )(x, od, osb, cinit, wod, wos, fnw, wup, cw, cb, wdn, finw)


def kernel(x_prompt, x_sample, cache_diff_k, cache_diff_v, cache_sb_k, cache_sb_v, state_conv, attn_norm_w, w_in, lambda_q1, lambda_k1, lambda_q2, lambda_k2, diff_subln_w, sb_norm_w, w_out, ffn_norm_w, w_up, conv_w, conv_b, w_down, rel_bias, final_norm_w):
    depth = w_in.shape[0]
    assert depth == 1, "single-layer step"
    bp, sp, d = x_prompt.shape
    bs, ss, _ = x_sample.shape
    past = cache_diff_k.shape[3]
    n_dh = cache_diff_k.shape[2]
    n_sh = cache_sb_k.shape[2]
    ff = w_down.shape[1]
    tile = ATT_TILE
    assert sp % tile == 0 and past % tile == 0 and tile % CHUNK == 0 and n_sh == 2 * n_dh
    assert ss % CARRY_ROWS == 0
    lam_init = 0.8 - 0.6 * math.exp(-0.3 * 0)

    loc = np.arange(tile)
    bkt_p = np.stack([_bucket_tiles(loc + tile, loc + tile), _bucket_tiles(loc + tile, loc)])
    far_bucket = int(_rel_bucket_np(np.array([-(tile + 1)]))[0])
    assert far_bucket == int(_rel_bucket_np(np.array([-(sp + past)]))[0])
    qpos_s = past + np.arange(ss)
    bkt_sp = _bucket_tiles(qpos_s, np.arange(past))
    bkt_sn = _bucket_tiles(qpos_s, qpos_s)

    row = lambda v: v.reshape(1, -1).astype(F32)
    bias_p, bias_sp, bias_sn, lam = _prep(
        rel_bias.astype(F32), row(lambda_q1[0]), row(lambda_k1[0]), row(lambda_q2[0]), row(lambda_k2[0]),
        jnp.asarray(bkt_p), jnp.asarray(bkt_sp), jnp.asarray(bkt_sn), lam_init, far_bucket)

    w_in_bf = w_in[0].astype(BF16)
    w_out_bf = w_out[0].astype(BF16)
    half = n_dh * LANES
    wod, wos = w_out_bf[:half], w_out_bf[half:]
    wup = w_up[0].astype(BF16)
    wdn = w_down[0].astype(BF16)
    anw, fnw, finw = row(attn_norm_w[0]), row(ffn_norm_w[0]), row(final_norm_w)
    subw = row(diff_subln_w[0]) * (1.0 - lam_init)
    nw2 = jnp.tile(row(sb_norm_w[0]), (1, 2))
    cw, cb = conv_w[0].astype(F32), row(conv_b[0])
    tri = np.triu(np.ones((tile, tile), np.float32), 1).T
    tri = jnp.asarray(np.concatenate([tri, tri], axis=0), BF16)

    def state8(st):
        return jnp.pad(st, ((0, 0), (CARRY_ROWS - st.shape[1], 0), (0, 0)))

    ts_p = min(512, sp)
    proj_p, p_dk, p_dv, p_sk, p_sv = _inproj(x_prompt, anw, w_in_bf, ts_p)
    od_p = _diff_attn(lam, proj_p, bias_p, subw, n_dh)
    os_p = _sb_attn(proj_p, tri, nw2, n_dh, 3 * n_dh)
    y_p, cv_p = _mlp(x_prompt, od_p, os_p, jnp.zeros((bp, CARRY_ROWS, 2 * ff), F32),
                     wod, wos, fnw, wup, cw, cb, wdn, finw, ts_p)

    proj_s, s_dk, s_dv, s_sk, s_sv = _inproj(x_sample, anw, w_in_bf, ss)
    od_s, os_s = _sample_attn(lam, proj_s, cache_diff_k, cache_diff_v,
                              jnp.swapaxes(cache_sb_k, 3, 4), jnp.swapaxes(cache_sb_v, 3, 4),
                              bias_sp, bias_sn, subw, nw2, tri, n_dh)
    y_s, cv_s = _mlp(x_sample, od_s, os_s, state8(state_conv[0]),
                     wod, wos, fnw, wup, cw, cb, wdn, finw, ss)

    tail = lambda cv: cv[None, :, CARRY_ROWS - 2:, :]
    return (y_p, y_s, p_dk, p_dv, p_sk, p_sv, tail(cv_p), s_dk, s_dv, s_sk, s_sv, tail(cv_s))
```

```python
import functools
import math

import numpy as np
import jax
import jax.numpy as jnp
from jax import lax
from jax.experimental import pallas as pl
from jax.experimental.pallas import tpu as pltpu

F32 = jnp.float32
BF16 = jnp.bfloat16

HEAD_DIM = 64
LANES = 128
CHUNK = 64
N_BUCKETS = 32
EPS = 1e-6
NEG = -1e30
LOG2E = 1.4426950408889634
SIGN_BIT = np.int32(-2 ** 31)
SB_ZERO_LOG2 = -160.0
ATT_TILE = 256
VMEM_LIMIT = 56 * 1024 * 1024


def _rel_bucket_np(rel):
    half = N_BUCKETS // 2
    max_exact = half // 2
    n = np.abs(rel)
    thresholds = (12, 16, 23, 32, 46, 64, 91)
    large = max_exact + sum((n >= t).astype(np.int64) for t in thresholds)
    return (np.where(rel > 0, half, 0) + np.where(n < max_exact, n, large)).astype(np.int32)


def _bucket_tiles(qpos, kpos):
    rel = kpos[None, :] - qpos[:, None]
    vis = (kpos[None, :] // CHUNK) <= (qpos[:, None] // CHUNK)
    return np.where(vis, _rel_bucket_np(rel), -1).astype(np.int32)


def _nt_dot(a, b):
    return lax.dot_general(a, b, (((1,), (1,)), ((), ())), preferred_element_type=F32)


def _dot(a, b):
    return jnp.dot(a, b, preferred_element_type=F32)


def _rms(x, w):
    return x * lax.rsqrt(jnp.mean(x * x, axis=-1, keepdims=True) + EPS) * w


def _prep_kernel(rb_ref, lq1_ref, lk1_ref, lq2_ref, lk2_ref, bp_ref, bsp_ref, bsn_ref,
                 bias_p_ref, bias_sp_ref, bias_sn_ref, lam_ref, *, lam_init, far_bucket):
    h = pl.program_id(0)
    pairs = ((bp_ref, bias_p_ref), (bsp_ref, bias_sp_ref), (bsn_ref, bias_sn_ref))
    for _, out in pairs:
        out[...] = jnp.full(out.shape, NEG, F32)

    def body(b, carry):
        val = rb_ref[b, h]
        for idx, out in pairs:
            out[0] = jnp.where(idx[...] == b, val, out[0])
        return carry

    lax.fori_loop(0, N_BUCKETS, body, 0)
    bias_p_ref[0] = bias_p_ref[0] - rb_ref[far_bucket, h]
    s1 = jnp.sum(lq1_ref[...] * lk1_ref[...], axis=-1, keepdims=True)
    s2 = jnp.sum(lq2_ref[...] * lk2_ref[...], axis=-1, keepdims=True)
    lam_ref[...] = jnp.exp(s1) - jnp.exp(s2) + lam_init


def _prep(rel_bias, lq1, lk1, lq2, lk2, bkt_p, bkt_sp, bkt_sn, lam_init, far_bucket):
    nh = rel_bias.shape[1]
    full = lambda a: pl.BlockSpec(a.shape, lambda h: (0,) * a.ndim)
    head = lambda a: pl.BlockSpec((1,) + a.shape, lambda h: (h,) + (0,) * a.ndim)
    vec = pl.BlockSpec((1, HEAD_DIM), lambda h: (0, 0))
    return pl.pallas_call(
        functools.partial(_prep_kernel, lam_init=lam_init, far_bucket=far_bucket),
        grid=(nh,),
        in_specs=[pl.BlockSpec(memory_space=pltpu.SMEM), vec, vec, vec, vec,
                  full(bkt_p), full(bkt_sp), full(bkt_sn)],
        out_specs=[head(bkt_p), head(bkt_sp), head(bkt_sn), pl.BlockSpec((1, 1), lambda h: (0, 0))],
        out_shape=[jax.ShapeDtypeStruct((nh,) + bkt_p.shape, F32),
                   jax.ShapeDtypeStruct((nh,) + bkt_sp.shape, F32),
                   jax.ShapeDtypeStruct((nh,) + bkt_sn.shape, F32),
                   jax.ShapeDtypeStruct((1, 1), F32)],
        name="prep",
    )(rel_bias, lq1, lk1, lq2, lk2, bkt_p, bkt_sp, bkt_sn)


def _inproj_kernel(x_ref, nw_ref, w_ref, proj_ref, dk_ref, dv_ref, sk_ref, sv_ref, *, width):
    h = _rms(x_ref[0], nw_ref[...]).astype(BF16)
    n_dh = width // LANES
    n_sh = width // HEAD_DIM
    for g in range(6):
        acc = _dot(h, w_ref[:, g * width:(g + 1) * width])
        proj_ref[0, :, g * width:(g + 1) * width] = acc.astype(BF16)
        if g in (1, 2):
            out = dk_ref if g == 1 else dv_ref
            for hh in range(n_dh):
                out[0, 0, hh] = acc[:, hh * LANES:(hh + 1) * LANES]
        if g in (4, 5):
            out = sk_ref if g == 4 else sv_ref
            for hh in range(n_sh):
                out[0, 0, hh] = acc[:, hh * HEAD_DIM:(hh + 1) * HEAD_DIM]


def _inproj(x, nw, w_in_bf, ts):
    b, s, d = x.shape
    cols = w_in_bf.shape[1]
    width = cols // 6
    n_dh, n_sh = width // LANES, width // HEAD_DIM
    kv = lambda nh, hd: pl.BlockSpec((1, 1, nh, ts, hd), lambda i, j: (0, i, 0, j, 0))
    return pl.pallas_call(
        functools.partial(_inproj_kernel, width=width),
        grid=(b, s // ts),
        in_specs=[pl.BlockSpec((1, ts, d), lambda i, j: (i, j, 0)),
                  pl.BlockSpec((1, d), lambda i, j: (0, 0)),
                  pl.BlockSpec((d, cols), lambda i, j: (0, 0))],
        out_specs=[pl.BlockSpec((1, ts, cols), lambda i, j: (i, j, 0)),
                   kv(n_dh, LANES), kv(n_dh, LANES), kv(n_sh, HEAD_DIM), kv(n_sh, HEAD_DIM)],
        out_shape=[jax.ShapeDtypeStruct((b, s, cols), BF16),
                   jax.ShapeDtypeStruct((1, b, n_dh, s, LANES), F32),
                   jax.ShapeDtypeStruct((1, b, n_dh, s, LANES), F32),
                   jax.ShapeDtypeStruct((1, b, n_sh, s, HEAD_DIM), F32),
                   jax.ShapeDtypeStruct((1, b, n_sh, s, HEAD_DIM), F32)],
        compiler_params=pltpu.CompilerParams(vmem_limit_bytes=VMEM_LIMIT),
        name="inproj",
    )(x, nw, w_in_bf)


def _diff_finish(a1, l1, a2, l2, lam, subw):
    o = a1 / l1 - lam * (a2 / l2)
    return _rms(o, subw)


def _pair_rms(o, lane, w):
    o2 = o * o
    tot = jnp.sum(o2, axis=-1, keepdims=True)
    lo = jnp.sum(jnp.where(lane < HEAD_DIM, o2, 0.0), axis=-1, keepdims=True)
    ms = jnp.where(lane < HEAD_DIM, lo, tot - lo) * (1.0 / HEAD_DIM)
    return o * lax.rsqrt(ms + EPS) * w


def _diff_kernel(lam_ref, q_ref, k_ref, v_ref, bias_ref, subw_ref, o_ref, *, tile, nq):
    lam = lam_ref[0, 0]
    lane = lax.broadcasted_iota(jnp.int32, (tile, LANES), 1)
    subw = subw_ref[...]
    rowmax = lambda t: jnp.max(t, axis=-1, keepdims=True)
    rowsum = lambda t: jnp.sum(t, axis=-1, keepdims=True)

    def scores(qi, half):
        q = q_ref[0, qi * tile:(qi + 1) * tile, :] * jnp.asarray(HEAD_DIM ** -0.5, BF16)
        in_half = (lane >= half * HEAD_DIM) & (lane < (half + 1) * HEAD_DIM)
        return _nt_dot(jnp.where(in_half, q, jnp.zeros_like(q)), k_ref[0, :(qi + 1) * tile, :])

    def attend(qi, s):
        parts = [s[:, qi * tile:] + bias_ref[0, 0]]
        if qi >= 1:
            parts.append(s[:, (qi - 1) * tile:qi * tile] + bias_ref[0, 1])
        if qi >= 2:
            parts.append(s[:, :(qi - 1) * tile])
        m = functools.reduce(jnp.maximum, [rowmax(t) for t in parts])
        ps = [jnp.exp(t - m) for t in parts]
        l = functools.reduce(jnp.add, [rowsum(t) for t in ps])
        pb = jnp.concatenate([t.astype(BF16) for t in reversed(ps)], axis=1)
        return _dot(pb, v_ref[0, :(qi + 1) * tile, :]), l

    rows = [(qi, half) for qi in range(nq) for half in range(2)]
    pending, done = {}, {}
    for t in range(len(rows) + 1):
        if t < len(rows):
            pending[rows[t]] = scores(*rows[t])
        if t >= 1:
            qi, half = rows[t - 1]
            done[half] = attend(qi, pending.pop((qi, half)))
            if half == 1:
                (a1, l1), (a2, l2) = done[0], done[1]
                o_ref[0, qi * tile:(qi + 1) * tile, :] = _diff_finish(a1, l1, a2, l2, lam, subw).astype(BF16)


def _diff_attn(lam, proj, bias_p, subw, n_heads):
    b, s, _ = proj.shape
    tile = ATT_TILE
    blk = lambda off: pl.BlockSpec((1, s, LANES), lambda i, h: (i, 0, off + h))
    return pl.pallas_call(
        functools.partial(_diff_kernel, tile=tile, nq=s // tile),
        grid=(b, n_heads),
        in_specs=[pl.BlockSpec(memory_space=pltpu.SMEM),
                  blk(0), blk(n_heads), blk(2 * n_heads),
                  pl.BlockSpec((1, 2, tile, tile), lambda i, h: (h, 0, 0, 0)),
                  pl.BlockSpec((1, LANES), lambda i, h: (0, 0))],
        out_specs=pl.BlockSpec((1, s, LANES), lambda i, h: (i, 0, h)),
        out_shape=jax.ShapeDtypeStruct((b, s, n_heads * LANES), BF16),
        compiler_params=pltpu.CompilerParams(vmem_limit_bytes=VMEM_LIMIT),
        name="diff_attn",
    )(lam, proj, proj, proj, bias_p, subw)


def _sb_logits(z2, tri2, tile, cmask=None, run=None, total=False):
    nb = z2.shape[1] // tile
    neg_abs = lax.bitcast_convert_type(lax.bitcast_convert_type(z2, jnp.int32) | SIGN_BIT, F32)
    sp = jnp.log2(1.0 + jnp.exp2(neg_abs))
    lb = jnp.minimum(z2, 0.0) - sp
    l1m = lb - z2
    xs = [None] * nb
    for j in range(nb - 1, -1, -1):
        cols = slice(j * tile, (j + 1) * tile)
        lj = l1m[:, cols]
        if cmask is not None and j == nb - 1:
            lj = jnp.where(cmask, lj, 0.0)
        hi = lj.astype(BF16)
        lo = (lj - hi.astype(F32)).astype(BF16)
        if tile % LANES == 0:
            suffix = _dot(jnp.concatenate([hi, lo], axis=1), tri2)
        else:
            suffix = _dot(hi, tri2) + _dot(lo, tri2)
        x = lb[:, cols] + suffix
        xs[j] = x if run is None else x + run
        if j > 0 or total:
            rs = jnp.sum(lj, axis=-1, keepdims=True)
            run = rs if run is None else run + rs
    return xs, run


def _sb_weights(xs, cmask=None):
    ws = [jnp.exp2(x) for x in xs]
    if cmask is not None:
        ws[-1] = jnp.where(cmask, ws[-1], 0.0)
    ws = [w.astype(BF16) for w in ws]
    return jnp.concatenate(ws, axis=1) if len(ws) > 1 else ws[0]


def _sb_kernel(q_ref, k_ref, v_ref, tri_ref, nw_ref, o_ref, *, tile, nq):
    lane = lax.broadcasted_iota(jnp.int32, (tile, LANES), 1)
    row = lax.broadcasted_iota(jnp.int32, (tile, tile), 0)
    colv = lax.broadcasted_iota(jnp.int32, (tile, tile), 1)
    cmask = colv < row
    tri = tri_ref[...]
    nw = nw_ref[...]

    scale = jnp.asarray(HEAD_DIM ** -0.5, BF16)

    def head_q(q, half):
        in_half = (lane >= half * HEAD_DIM) & (lane < (half + 1) * HEAD_DIM)
        return jnp.where(in_half, q, jnp.zeros_like(q))

    def finish(halves):
        o = jnp.where(lane < HEAD_DIM, halves[0], halves[1])
        return _pair_rms(o, lane, nw).astype(BF16)

    first = lambda qi: max(qi - 1, 0) * tile
    rows = [(qi, half) for qi in range(nq) for half in range(2)]
    z2s, xss, halves = {}, {}, {}
    worst = None
    for t in range(len(rows) + 2):
        if t < len(rows):
            qi, half = rows[t]
            q = q_ref[0, qi * tile:(qi + 1) * tile, :] * scale
            z2s[rows[t]] = _nt_dot(head_q(q, half), k_ref[0, first(qi):(qi + 1) * tile, :]) * LOG2E
        if 1 <= t <= len(rows):
            qi, half = rows[t - 1]
            xss[qi, half], run = _sb_logits(z2s.pop((qi, half)), tri, tile, cmask, None, total=qi >= 2)
            if qi >= 2:
                worst = run if worst is None else jnp.maximum(worst, run)
        if t >= 2:
            qi, half = rows[t - 2]
            halves[half] = _dot(_sb_weights(xss.pop((qi, half)), cmask), v_ref[0, first(qi):(qi + 1) * tile, :])
            if half == 1:
                o_ref[0, qi * tile:(qi + 1) * tile, :] = finish(halves)

    if worst is None:
        return

    @pl.when(jnp.max(worst) >= SB_ZERO_LOG2)
    def _():
        def q_body(qi, carry):
            q0 = pl.multiple_of(qi * tile, tile)
            q = q_ref[0, pl.ds(q0, tile), :] * scale
            full = []
            for half in range(2):
                qm = head_q(q, half)
                xs, run = _sb_logits(_nt_dot(qm, k_ref[0, pl.ds(q0, tile), :]) * LOG2E,
                                     tri, tile, cmask, None, total=True)
                acc = _dot(_sb_weights(xs, cmask), v_ref[0, pl.ds(q0, tile), :])

                def kb_body(it, c):
                    acc, run = c
                    k0 = pl.multiple_of((qi - 1 - it) * tile, tile)
                    xs, run = _sb_logits(_nt_dot(qm, k_ref[0, pl.ds(k0, tile), :]) * LOG2E,
                                         tri, tile, None, run, total=True)
                    return acc + _dot(_sb_weights(xs), v_ref[0, pl.ds(k0, tile), :]), run

                acc, _ = lax.fori_loop(0, qi, kb_body, (acc, run))
                full.append(acc)
            o_ref[0, pl.ds(q0, tile), :] = finish(full)
            return carry

        lax.fori_loop(2, nq, q_body, 0)


def _sb_attn(proj, tri, nw2, n_pairs, off):
    b, s, _ = proj.shape
    tile = ATT_TILE
    blk = lambda o: pl.BlockSpec((1, s, LANES), lambda i, p: (i, 0, o + p))
    return pl.pallas_call(
        functools.partial(_sb_kernel, tile=tile, nq=s // tile),
        grid=(b, n_pairs),
        in_specs=[blk(off), blk(off + n_pairs), blk(off + 2 * n_pairs),
                  pl.BlockSpec(tri.shape, lambda i, p: (0, 0)),
                  pl.BlockSpec((1, LANES), lambda i, p: (0, 0))],
        out_specs=pl.BlockSpec((1, s, LANES), lambda i, p: (i, 0, p)),
        out_shape=jax.ShapeDtypeStruct((b, s, n_pairs * LANES), BF16),
        compiler_params=pltpu.CompilerParams(vmem_limit_bytes=VMEM_LIMIT),
        name="sb_attn",
    )(proj, proj, proj, tri, nw2)


def _sample_attn_kernel(lam_ref, qd_ref, kd_ref, vd_ref, qs_ref, ks_ref, vs_ref,
                        cdk_ref, cdv_ref, csk_ref, csv_ref, bsp_ref, bsn_ref,
                        subw_ref, nw_ref, tri_ref, od_ref, os_ref, *, sq, past, tile):
    lam = lam_ref[0, 0]
    scale = jnp.asarray(HEAD_DIM ** -0.5, BF16)
    lane = lax.broadcasted_iota(jnp.int32, (sq, LANES), 1)

    q = qd_ref[0] * scale
    kc = cdk_ref[0, 0, 0].astype(BF16)
    vc = cdv_ref[0, 0, 0].astype(BF16)
    kn = kd_ref[0]
    vn = vd_ref[0]
    bias_c = bsp_ref[0]
    bias_n = bsn_ref[0]
    outs = []
    for qm in (jnp.where(lane < HEAD_DIM, q, jnp.zeros_like(q)),
               jnp.where(lane >= HEAD_DIM, q, jnp.zeros_like(q))):
        sc = _nt_dot(qm, kc) + bias_c
        sn = _nt_dot(qm, kn) + bias_n
        m = jnp.maximum(jnp.max(sc, axis=-1, keepdims=True), jnp.max(sn, axis=-1, keepdims=True))
        pc = jnp.exp(sc - m)
        pn = jnp.exp(sn - m)
        l = jnp.sum(pc, axis=-1, keepdims=True) + jnp.sum(pn, axis=-1, keepdims=True)
        outs.append((_dot(pc.astype(BF16), vc) + _dot(pn.astype(BF16), vn), l))
    (a1, l1), (a2, l2) = outs
    od_ref[0] = _diff_finish(a1, l1, a2, l2, lam, subw_ref[...]).astype(BF16)

    row = lax.broadcasted_iota(jnp.int32, (sq, sq), 0)
    colv = lax.broadcasted_iota(jnp.int32, (sq, sq), 1)
    cmask = colv < row
    tri2 = tri_ref[...]
    tri_new = tri2[:sq, :sq]
    qs = qs_ref[0] * scale
    heads = [slice(hh * HEAD_DIM, (hh + 1) * HEAD_DIM) for hh in range(2)]
    z_new = [_nt_dot(qs[:, sl], ks_ref[0][:, sl]) * LOG2E for sl in heads]
    z_old = [_dot(qs[:, sl], csk_ref[0, 0, hh].astype(BF16)) * LOG2E for hh, sl in enumerate(heads)]
    halves = []
    for hh, sl in enumerate(heads):
        x_new, run = _sb_logits(z_new[hh], tri_new, sq, cmask, None, total=True)
        x_old, _ = _sb_logits(z_old[hh], tri2, tile, None, run)
        halves.append(_dot(_sb_weights(x_new, cmask), vs_ref[0][:, sl])
                      + _nt_dot(_sb_weights(x_old), csv_ref[0, 0, hh].astype(BF16)))
    o = jnp.concatenate(halves, axis=-1)
    os_ref[0] = _pair_rms(o, lane, nw_ref[...]).astype(BF16)


def _sample_attn(lam, proj, cdk, cdv, csk, csv, bias_sp, bias_sn, subw, nw2, tri, n_dh):
    b, sq, _ = proj.shape
    past = cdk.shape[3]
    tile = ATT_TILE
    blk = lambda off: pl.BlockSpec((1, sq, LANES), lambda i, h: (i, 0, off + h))
    cd = pl.BlockSpec((1, 1, 1, past, LANES), lambda i, h: (0, i, h, 0, 0))
    cs = pl.BlockSpec((1, 1, 2, HEAD_DIM, past), lambda i, h: (0, i, h, 0, 0))
    const2 = lambda a: pl.BlockSpec(a.shape, lambda i, h: (0, 0))
    out = pl.BlockSpec((1, sq, LANES), lambda i, h: (i, 0, h))
    return pl.pallas_call(
        functools.partial(_sample_attn_kernel, sq=sq, past=past, tile=tile),
        grid=(b, n_dh),
        in_specs=[pl.BlockSpec(memory_space=pltpu.SMEM),
                  blk(0), blk(n_dh), blk(2 * n_dh), blk(3 * n_dh), blk(4 * n_dh), blk(5 * n_dh),
                  cd, cd, cs, cs,
                  pl.BlockSpec((1, sq, past), lambda i, h: (h, 0, 0)),
                  pl.BlockSpec((1, sq, sq), lambda i, h: (h, 0, 0)),
                  const2(subw), const2(nw2), const2(tri)],
        out_specs=[out, out],
        out_shape=[jax.ShapeDtypeStruct((b, sq, n_dh * LANES), BF16)] * 2,
        compiler_params=pltpu.CompilerParams(vmem_limit_bytes=VMEM_LIMIT),
        name="sample_attn",
    )(lam, proj, proj, proj, proj, proj, proj, cdk, cdv, csk, csv, bias_sp, bias_sn, subw, nw2, tri)


CARRY_ROWS = 8


def _mlp_kernel(x_ref, od_ref, os_ref, cinit_ref, wod_ref, wos_ref, fnw_ref, wup_ref, cw_ref, cb_ref,
                wdn_ref, finw_ref, y_ref, nconv_ref, carry_ref, *, ts, ff, fc):
    @pl.when(pl.program_id(1) == 0)
    def _():
        carry_ref[...] = cinit_ref[0]

    x1 = x_ref[0] + _dot(od_ref[0], wod_ref[...]) + _dot(os_ref[0], wos_ref[...])
    h2 = _rms(x1, fnw_ref[...]).astype(BF16)
    groups = ts // CARRY_ROWS
    sub = lax.broadcasted_iota(jnp.int32, (groups, CARRY_ROWS, fc), 1)
    nc = ff // fc

    def up(c):
        return tuple(_dot(h2, wup_ref[:, c0:c0 + fc]) for c0 in (c * fc, ff + c * fc))

    def conv(u, c0):
        cols = slice(c0, c0 + fc)
        u3 = u.reshape(groups, CARRY_ROWS, fc)
        before = carry_ref[:, cols]

        def shifted(k):
            rot = pltpu.roll(u3, k, 1)
            above = pltpu.roll(before, k, 0)[None]
            if groups > 1:
                above = jnp.concatenate([above, rot[:-1]], axis=0)
            return jnp.where(sub < k, above, rot).reshape(ts, fc)

        out = (cb_ref[:, cols] + shifted(2) * cw_ref[0:1, cols] + shifted(1) * cw_ref[1:2, cols]
               + u * cw_ref[2:3, cols])
        tail = u[ts - CARRY_ROWS:, :]
        carry_ref[:, cols] = tail
        nconv_ref[0, :, cols] = tail
        return out

    acc = jnp.zeros_like(x1)
    u = up(0)
    acts = []
    for c in range(nc):
        u_next = up(c + 1) if c + 1 < nc else None
        g = conv(u[0], c * fc)
        v = conv(u[1], ff + c * fc)
        acts.append((g / (1.0 + jnp.exp(-g)) * v).astype(BF16))
        if len(acts) == 2 or c == nc - 1:
            act = jnp.concatenate(acts, axis=1) if len(acts) > 1 else acts[0]
            acc = acc + _dot(act, wdn_ref[(c + 1 - len(acts)) * fc:(c + 1) * fc, :])
            acts = []
        u = u_next
    y_ref[0] = _rms(x1 + acc, finw_ref[...])


def _mlp(x, od, osb, cinit, wod, wos, fnw, wup, cw, cb, wdn, finw, ts):
    b, s, d = x.shape
    ff = wdn.shape[0]
    half = od.shape[2]
    fc = 256
    const = lambda a: pl.BlockSpec(a.shape, lambda i, j: (0, 0))
    tok = lambda w: pl.BlockSpec((1, ts, w), lambda i, j: (i, j, 0))
    state = pl.BlockSpec((1, CARRY_ROWS, 2 * ff), lambda i, j: (i, 0, 0))
    return pl.pallas_call(
        functools.partial(_mlp_kernel, ts=ts, ff=ff, fc=fc),
        grid=(b, s // ts),
        in_specs=[tok(d), tok(half), tok(half), state, const(wod), const(wos), const(fnw), const(wup),
                  const(cw), const(cb), const(wdn), const(finw)],
        out_specs=[tok(d), state],
        out_shape=[jax.ShapeDtypeStruct((b, s, d), F32),
                   jax.ShapeDtypeStruct((b, CARRY_ROWS, 2 * ff), F32)],
        scratch_shapes=[pltpu.VMEM((CARRY_ROWS, 2 * ff), F32)],
        compiler_params=pltpu.CompilerParams(dimension_semantics=("arbitrary", "arbitrary"),
                                             vmem_limit_bytes=VMEM_LIMIT),
        name="mlp",
    )(x, od, osb, cinit, wod, wos, fnw, wup, cw, cb, wdn, finw)


def kernel(x_prompt, x_sample, cache_diff_k, cache_diff_v, cache_sb_k, cache_sb_v, state_conv, attn_norm_w, w_in, lambda_q1, lambda_k1, lambda_q2, lambda_k2, diff_subln_w, sb_norm_w, w_out, ffn_norm_w, w_up, conv_w, conv_b, w_down, rel_bias, final_norm_w):
    depth = w_in.shape[0]
    assert depth == 1, "single-layer step"
    bp, sp, d = x_prompt.shape
    bs, ss, _ = x_sample.shape
    past = cache_diff_k.shape[3]
    n_dh = cache_diff_k.shape[2]
    n_sh = cache_sb_k.shape[2]
    ff = w_down.shape[1]
    tile = ATT_TILE
    assert sp % tile == 0 and past % tile == 0 and tile % CHUNK == 0 and n_sh == 2 * n_dh
    assert ss % CARRY_ROWS == 0
    lam_init = 0.8 - 0.6 * math.exp(-0.3 * 0)

    loc = np.arange(tile)
    bkt_p = np.stack([_bucket_tiles(loc + tile, loc + tile), _bucket_tiles(loc + tile, loc)])
    far_bucket = int(_rel_bucket_np(np.array([-(tile + 1)]))[0])
    assert far_bucket == int(_rel_bucket_np(np.array([-(sp + past)]))[0])
    qpos_s = past + np.arange(ss)
    bkt_sp = _bucket_tiles(qpos_s, np.arange(past))
    bkt_sn = _bucket_tiles(qpos_s, qpos_s)

    row = lambda v: v.reshape(1, -1).astype(F32)
    bias_p, bias_sp, bias_sn, lam = _prep(
        rel_bias.astype(F32), row(lambda_q1[0]), row(lambda_k1[0]), row(lambda_q2[0]), row(lambda_k2[0]),
        jnp.asarray(bkt_p), jnp.asarray(bkt_sp), jnp.asarray(bkt_sn), lam_init, far_bucket)

    w_in_bf = w_in[0].astype(BF16)
    w_out_bf = w_out[0].astype(BF16)
    half = n_dh * LANES
    wod, wos = w_out_bf[:half], w_out_bf[half:]
    wup = w_up[0].astype(BF16)
    wdn = w_down[0].astype(BF16)
    anw, fnw, finw = row(attn_norm_w[0]), row(ffn_norm_w[0]), row(final_norm_w)
    subw = row(diff_subln_w[0]) * (1.0 - lam_init)
    nw2 = jnp.tile(row(sb_norm_w[0]), (1, 2))
    cw, cb = conv_w[0].astype(F32), row(conv_b[0])
    tri = np.triu(np.ones((tile, tile), np.float32), 1).T
    tri = jnp.asarray(np.concatenate([tri, tri], axis=0), BF16)

    def state8(st):
        return jnp.pad(st, ((0, 0), (CARRY_ROWS - st.shape[1], 0), (0, 0)))

    ts_p = min(512, sp)
    proj_p, p_dk, p_dv, p_sk, p_sv = _inproj(x_prompt, anw, w_in_bf, ts_p)
    od_p = _diff_attn(lam, proj_p, bias_p, subw, n_dh)
    os_p = _sb_attn(proj_p, tri, nw2, n_dh, 3 * n_dh)
    y_p, cv_p = _mlp(x_prompt, od_p, os_p, jnp.zeros((bp, CARRY_ROWS, 2 * ff), F32),
                     wod, wos, fnw, wup, cw, cb, wdn, finw, ts_p)

    proj_s, s_dk, s_dv, s_sk, s_sv = _inproj(x_sample, anw, w_in_bf, ss)
    od_s, os_s = _sample_attn(lam, proj_s, cache_diff_k, cache_diff_v,
                              jnp.swapaxes(cache_sb_k, 3, 4), jnp.swapaxes(cache_sb_v, 3, 4),
                              bias_sp, bias_sn, subw, nw2, tri, n_dh)
    y_s, cv_s = _mlp(x_sample, od_s, os_s, state8(state_conv[0]),
                     wod, wos, fnw, wup, cw, cb, wdn, finw, ss)

    tail = lambda cv: cv[None, :, CARRY_ROWS - 2:, :]
    return (y_p, y_s, p_dk, p_dv, p_sk, p_sv, tail(cv_p), s_dk, s_dv, s_sk, s_sv, tail(cv_s))
```

```python
import functools
import math

import numpy as np
import jax
import jax.numpy as jnp
from jax import lax
from jax.experimental import pallas as pl
from jax.experimental.pallas import tpu as pltpu

F32 = jnp.float32
BF16 = jnp.bfloat16

HEAD_DIM = 64
LANES = 128
CHUNK = 64
N_BUCKETS = 32
EPS = 1e-6
NEG = -1e30
LOG2E = 1.4426950408889634
SIGN_BIT = np.int32(-2 ** 31)
SB_ZERO_LOG2 = -160.0
ATT_TILE = 256
VMEM_LIMIT = 60 * 1024 * 1024


def _rel_bucket_np(rel):
    half = N_BUCKETS // 2
    max_exact = half // 2
    n = np.abs(rel)
    thresholds = (12, 16, 23, 32, 46, 64, 91)
    large = max_exact + sum((n >= t).astype(np.int64) for t in thresholds)
    return (np.where(rel > 0, half, 0) + np.where(n < max_exact, n, large)).astype(np.int32)


def _bucket_tiles(qpos, kpos):
    rel = kpos[None, :] - qpos[:, None]
    vis = (kpos[None, :] // CHUNK) <= (qpos[:, None] // CHUNK)
    return np.where(vis, _rel_bucket_np(rel), -1).astype(np.int32)


def _nt_dot(a, b):
    return lax.dot_general(a, b, (((1,), (1,)), ((), ())), preferred_element_type=F32)


def _dot(a, b):
    return jnp.dot(a, b, preferred_element_type=F32)


def _rms(x, w):
    return x * lax.rsqrt(jnp.mean(x * x, axis=-1, keepdims=True) + EPS) * w


def _prep_kernel(rb_ref, lq1_ref, lk1_ref, lq2_ref, lk2_ref, bp_ref, bsp_ref, bsn_ref,
                 bias_p_ref, bias_sp_ref, bias_sn_ref, lam_ref, *, lam_init, far_bucket):
    h = pl.program_id(0)
    pairs = ((bp_ref, bias_p_ref), (bsp_ref, bias_sp_ref), (bsn_ref, bias_sn_ref))
    for _, out in pairs:
        out[...] = jnp.full(out.shape, NEG, F32)

    def body(b, carry):
        val = rb_ref[b, h]
        for idx, out in pairs:
            out[0] = jnp.where(idx[...] == b, val, out[0])
        return carry

    lax.fori_loop(0, N_BUCKETS, body, 0)
    bias_p_ref[0] = bias_p_ref[0] - rb_ref[far_bucket, h]
    s1 = jnp.sum(lq1_ref[...] * lk1_ref[...], axis=-1, keepdims=True)
    s2 = jnp.sum(lq2_ref[...] * lk2_ref[...], axis=-1, keepdims=True)
    lam_ref[...] = jnp.exp(s1) - jnp.exp(s2) + lam_init


def _prep(rel_bias, lq1, lk1, lq2, lk2, bkt_p, bkt_sp, bkt_sn, lam_init, far_bucket):
    nh = rel_bias.shape[1]
    full = lambda a: pl.BlockSpec(a.shape, lambda h: (0,) * a.ndim)
    head = lambda a: pl.BlockSpec((1,) + a.shape, lambda h: (h,) + (0,) * a.ndim)
    vec = pl.BlockSpec((1, HEAD_DIM), lambda h: (0, 0))
    return pl.pallas_call(
        functools.partial(_prep_kernel, lam_init=lam_init, far_bucket=far_bucket),
        grid=(nh,),
        in_specs=[pl.BlockSpec(memory_space=pltpu.SMEM), vec, vec, vec, vec,
                  full(bkt_p), full(bkt_sp), full(bkt_sn)],
        out_specs=[head(bkt_p), head(bkt_sp), head(bkt_sn), pl.BlockSpec((1, 1), lambda h: (0, 0))],
        out_shape=[jax.ShapeDtypeStruct((nh,) + bkt_p.shape, F32),
                   jax.ShapeDtypeStruct((nh,) + bkt_sp.shape, F32),
                   jax.ShapeDtypeStruct((nh,) + bkt_sn.shape, F32),
                   jax.ShapeDtypeStruct((1, 1), F32)],
        name="prep",
    )(rel_bias, lq1, lk1, lq2, lk2, bkt_p, bkt_sp, bkt_sn)


def _inproj_kernel(x_ref, nw_ref, w_ref, proj_ref, dk_ref, dv_ref, sk_ref, sv_ref, *, width):
    h = _rms(x_ref[0], nw_ref[...]).astype(BF16)
    n_dh = width // LANES
    n_sh = width // HEAD_DIM
    for g in range(6):
        acc = _dot(h, w_ref[:, g * width:(g + 1) * width])
        proj_ref[0, :, g * width:(g + 1) * width] = acc.astype(BF16)
        if g in (1, 2):
            out = dk_ref if g == 1 else dv_ref
            for hh in range(n_dh):
                out[0, 0, hh] = acc[:, hh * LANES:(hh + 1) * LANES]
        if g in (4, 5):
            out = sk_ref if g == 4 else sv_ref
            for hh in range(n_sh):
                out[0, 0, hh] = acc[:, hh * HEAD_DIM:(hh + 1) * HEAD_DIM]


def _inproj(x, nw, w_in_bf, ts):
    b, s, d = x.shape
    cols = w_in_bf.shape[1]
    width = cols // 6
    n_dh, n_sh = width // LANES, width // HEAD_DIM
    kv = lambda nh, hd: pl.BlockSpec((1, 1, nh, ts, hd), lambda i, j: (0, i, 0, j, 0))
    return pl.pallas_call(
        functools.partial(_inproj_kernel, width=width),
        grid=(b, s // ts),
        in_specs=[pl.BlockSpec((1, ts, d), lambda i, j: (i, j, 0)),
                  pl.BlockSpec((1, d), lambda i, j: (0, 0)),
                  pl.BlockSpec((d, cols), lambda i, j: (0, 0))],
        out_specs=[pl.BlockSpec((1, ts, cols), lambda i, j: (i, j, 0)),
                   kv(n_dh, LANES), kv(n_dh, LANES), kv(n_sh, HEAD_DIM), kv(n_sh, HEAD_DIM)],
        out_shape=[jax.ShapeDtypeStruct((b, s, cols), BF16),
                   jax.ShapeDtypeStruct((1, b, n_dh, s, LANES), F32),
                   jax.ShapeDtypeStruct((1, b, n_dh, s, LANES), F32),
                   jax.ShapeDtypeStruct((1, b, n_sh, s, HEAD_DIM), F32),
                   jax.ShapeDtypeStruct((1, b, n_sh, s, HEAD_DIM), F32)],
        compiler_params=pltpu.CompilerParams(vmem_limit_bytes=VMEM_LIMIT),
        name="inproj",
    )(x, nw, w_in_bf)


def _diff_finish(a1, l1, a2, l2, lam, subw):
    o = a1 / l1 - lam * (a2 / l2)
    return _rms(o, subw)


def _pair_rms(o, lane, w):
    o2 = o * o
    tot = jnp.sum(o2, axis=-1, keepdims=True)
    lo = jnp.sum(jnp.where(lane < HEAD_DIM, o2, 0.0), axis=-1, keepdims=True)
    ms = jnp.where(lane < HEAD_DIM, lo, tot - lo) * (1.0 / HEAD_DIM)
    return o * lax.rsqrt(ms + EPS) * w


def _diff_kernel(lam_ref, q_ref, k_ref, v_ref, bias_ref, subw_ref, o_ref, *, tile, nq):
    lam = lam_ref[0, 0]
    lane = lax.broadcasted_iota(jnp.int32, (tile, LANES), 1)
    subw = subw_ref[...]
    rowmax = lambda t: jnp.max(t, axis=-1, keepdims=True)
    rowsum = lambda t: jnp.sum(t, axis=-1, keepdims=True)

    def scores(qi, half):
        q = q_ref[0, qi * tile:(qi + 1) * tile, :] * jnp.asarray(HEAD_DIM ** -0.5, BF16)
        in_half = (lane >= half * HEAD_DIM) & (lane < (half + 1) * HEAD_DIM)
        return _nt_dot(jnp.where(in_half, q, jnp.zeros_like(q)), k_ref[0, :(qi + 1) * tile, :])

    def attend(qi, s):
        parts = [s[:, qi * tile:] + bias_ref[0, 0]]
        if qi >= 1:
            parts.append(s[:, (qi - 1) * tile:qi * tile] + bias_ref[0, 1])
        if qi >= 2:
            parts.append(s[:, :(qi - 1) * tile])
        m = functools.reduce(jnp.maximum, [rowmax(t) for t in parts])
        ps = [jnp.exp(t - m) for t in parts]
        l = functools.reduce(jnp.add, [rowsum(t) for t in ps])
        pb = jnp.concatenate([t.astype(BF16) for t in reversed(ps)], axis=1)
        return _dot(pb, v_ref[0, :(qi + 1) * tile, :]), l

    rows = [(qi, half) for qi in range(nq) for half in range(2)]
    pending, done = {}, {}
    for t in range(len(rows) + 1):
        if t < len(rows):
            pending[rows[t]] = scores(*rows[t])
        if t >= 1:
            qi, half = rows[t - 1]
            done[half] = attend(qi, pending.pop((qi, half)))
            if half == 1:
                (a1, l1), (a2, l2) = done[0], done[1]
                o_ref[0, qi * tile:(qi + 1) * tile, :] = _diff_finish(a1, l1, a2, l2, lam, subw).astype(BF16)


def _diff_attn(lam, proj, bias_p, subw, n_heads):
    b, s, _ = proj.shape
    tile = ATT_TILE
    blk = lambda off: pl.BlockSpec((1, s, LANES), lambda i, h: (i, 0, off + h))
    return pl.pallas_call(
        functools.partial(_diff_kernel, tile=tile, nq=s // tile),
        grid=(b, n_heads),
        in_specs=[pl.BlockSpec(memory_space=pltpu.SMEM),
                  blk(0), blk(n_heads), blk(2 * n_heads),
                  pl.BlockSpec((1, 2, tile, tile), lambda i, h: (h, 0, 0, 0)),
                  pl.BlockSpec((1, LANES), lambda i, h: (0, 0))],
        out_specs=pl.BlockSpec((1, s, LANES), lambda i, h: (i, 0, h)),
        out_shape=jax.ShapeDtypeStruct((b, s, n_heads * LANES), BF16),
        compiler_params=pltpu.CompilerParams(vmem_limit_bytes=VMEM_LIMIT),
        name="diff_attn",
    )(lam, proj, proj, proj, bias_p, subw)


def _sb_logits(z2, tri2, tile, cmask=None, run=None, total=False):
    nb = z2.shape[1] // tile
    neg_abs = lax.bitcast_convert_type(lax.bitcast_convert_type(z2, jnp.int32) | SIGN_BIT, F32)
    sp = jnp.log2(1.0 + jnp.exp2(neg_abs))
    lb = jnp.minimum(z2, 0.0) - sp
    l1m = lb - z2
    xs = [None] * nb
    for j in range(nb - 1, -1, -1):
        cols = slice(j * tile, (j + 1) * tile)
        lj = l1m[:, cols]
        if cmask is not None and j == nb - 1:
            lj = jnp.where(cmask, lj, 0.0)
        hi = lj.astype(BF16)
        lo = (lj - hi.astype(F32)).astype(BF16)
        if tile % LANES == 0:
            suffix = _dot(jnp.concatenate([hi, lo], axis=1), tri2)
        else:
            suffix = _dot(hi, tri2) + _dot(lo, tri2)
        x = lb[:, cols] + suffix
        xs[j] = x if run is None else x + run
        if j > 0 or total:
            rs = jnp.sum(lj, axis=-1, keepdims=True)
            run = rs if run is None else run + rs
    return xs, run


def _sb_weights(xs, cmask=None):
    ws = [jnp.exp2(x) for x in xs]
    if cmask is not None:
        ws[-1] = jnp.where(cmask, ws[-1], 0.0)
    ws = [w.astype(BF16) for w in ws]
    return jnp.concatenate(ws, axis=1) if len(ws) > 1 else ws[0]


def _sb_kernel(q_ref, k_ref, v_ref, tri_ref, nw_ref, o_ref, *, tile, nq):
    lane = lax.broadcasted_iota(jnp.int32, (tile, LANES), 1)
    row = lax.broadcasted_iota(jnp.int32, (tile, tile), 0)
    colv = lax.broadcasted_iota(jnp.int32, (tile, tile), 1)
    cmask = colv < row
    tri = tri_ref[...]
    nw = nw_ref[...]

    scale = jnp.asarray(HEAD_DIM ** -0.5, BF16)

    def head_q(q, half):
        in_half = (lane >= half * HEAD_DIM) & (lane < (half + 1) * HEAD_DIM)
        return jnp.where(in_half, q, jnp.zeros_like(q))

    def finish(halves):
        o = jnp.where(lane < HEAD_DIM, halves[0], halves[1])
        return _pair_rms(o, lane, nw).astype(BF16)

    first = lambda qi: max(qi - 1, 0) * tile
    rows = [(qi, half) for qi in range(nq) for half in range(2)]
    z2s, xss, halves = {}, {}, {}
    worst = None
    for t in range(len(rows) + 2):
        if t < len(rows):
            qi, half = rows[t]
            q = q_ref[0, qi * tile:(qi + 1) * tile, :] * scale
            z2s[rows[t]] = _nt_dot(head_q(q, half), k_ref[0, first(qi):(qi + 1) * tile, :]) * LOG2E
        if 1 <= t <= len(rows):
            qi, half = rows[t - 1]
            xss[qi, half], run = _sb_logits(z2s.pop((qi, half)), tri, tile, cmask, None, total=qi >= 2)
            if qi >= 2:
                worst = run if worst is None else jnp.maximum(worst, run)
        if t >= 2:
            qi, half = rows[t - 2]
            halves[half] = _dot(_sb_weights(xss.pop((qi, half)), cmask), v_ref[0, first(qi):(qi + 1) * tile, :])
            if half == 1:
                o_ref[0, qi * tile:(qi + 1) * tile, :] = finish(halves)

    if worst is None:
        return

    @pl.when(jnp.max(worst) >= SB_ZERO_LOG2)
    def _():
        def q_body(qi, carry):
            q0 = pl.multiple_of(qi * tile, tile)
            q = q_ref[0, pl.ds(q0, tile), :] * scale
            full = []
            for half in range(2):
                qm = head_q(q, half)
                xs, run = _sb_logits(_nt_dot(qm, k_ref[0, pl.ds(q0, tile), :]) * LOG2E,
                                     tri, tile, cmask, None, total=True)
                acc = _dot(_sb_weights(xs, cmask), v_ref[0, pl.ds(q0, tile), :])

                def kb_body(it, c):
                    acc, run = c
                    k0 = pl.multiple_of((qi - 1 - it) * tile, tile)
                    xs, run = _sb_logits(_nt_dot(qm, k_ref[0, pl.ds(k0, tile), :]) * LOG2E,
                                         tri, tile, None, run, total=True)
                    return acc + _dot(_sb_weights(xs), v_ref[0, pl.ds(k0, tile), :]), run

                acc, _ = lax.fori_loop(0, qi, kb_body, (acc, run))
                full.append(acc)
            o_ref[0, pl.ds(q0, tile), :] = finish(full)
            return carry

        lax.fori_loop(2, nq, q_body, 0)


def _sb_attn(proj, tri, nw2, n_pairs, off):
    b, s, _ = proj.shape
    tile = ATT_TILE
    blk = lambda o: pl.BlockSpec((1, s, LANES), lambda i, p: (i, 0, o + p))
    return pl.pallas_call(
        functools.partial(_sb_kernel, tile=tile, nq=s // tile),
        grid=(b, n_pairs),
        in_specs=[blk(off), blk(off + n_pairs), blk(off + 2 * n_pairs),
                  pl.BlockSpec(tri.shape, lambda i, p: (0, 0)),
                  pl.BlockSpec((1, LANES), lambda i, p: (0, 0))],
        out_specs=pl.BlockSpec((1, s, LANES), lambda i, p: (i, 0, p)),
        out_shape=jax.ShapeDtypeStruct((b, s, n_pairs * LANES), BF16),
        compiler_params=pltpu.CompilerParams(vmem_limit_bytes=VMEM_LIMIT),
        name="sb_attn",
    )(proj, proj, proj, tri, nw2)


def _sample_attn_kernel(lam_ref, qd_ref, kd_ref, vd_ref, qs_ref, ks_ref, vs_ref,
                        cdk_ref, cdv_ref, csk_ref, csv_ref, bsp_ref, bsn_ref,
                        subw_ref, nw_ref, tri_ref, od_ref, os_ref, *, sq, past, tile):
    lam = lam_ref[0, 0]
    scale = jnp.asarray(HEAD_DIM ** -0.5, BF16)
    lane = lax.broadcasted_iota(jnp.int32, (sq, LANES), 1)

    q = qd_ref[0] * scale
    kc = cdk_ref[0, 0, 0].astype(BF16)
    vc = cdv_ref[0, 0, 0].astype(BF16)
    kn = kd_ref[0]
    vn = vd_ref[0]
    bias_c = bsp_ref[0]
    bias_n = bsn_ref[0]
    outs = []
    for qm in (jnp.where(lane < HEAD_DIM, q, jnp.zeros_like(q)),
               jnp.where(lane >= HEAD_DIM, q, jnp.zeros_like(q))):
        sc = _nt_dot(qm, kc) + bias_c
        sn = _nt_dot(qm, kn) + bias_n
        m = jnp.maximum(jnp.max(sc, axis=-1, keepdims=True), jnp.max(sn, axis=-1, keepdims=True))
        pc = jnp.exp(sc - m)
        pn = jnp.exp(sn - m)
        l = jnp.sum(pc, axis=-1, keepdims=True) + jnp.sum(pn, axis=-1, keepdims=True)
        outs.append((_dot(pc.astype(BF16), vc) + _dot(pn.astype(BF16), vn), l))
    (a1, l1), (a2, l2) = outs
    od_ref[0] = _diff_finish(a1, l1, a2, l2, lam, subw_ref[...]).astype(BF16)

    row = lax.broadcasted_iota(jnp.int32, (sq, sq), 0)
    colv = lax.broadcasted_iota(jnp.int32, (sq, sq), 1)
    cmask = colv < row
    tri2 = tri_ref[...]
    tri_new = tri2[:sq, :sq]
    qs = qs_ref[0] * scale
    heads = [slice(hh * HEAD_DIM, (hh + 1) * HEAD_DIM) for hh in range(2)]
    z_new = [_nt_dot(qs[:, sl], ks_ref[0][:, sl]) * LOG2E for sl in heads]
    z_old = [_dot(qs[:, sl], csk_ref[0, 0, hh].astype(BF16)) * LOG2E for hh, sl in enumerate(heads)]
    halves = []
    for hh, sl in enumerate(heads):
        x_new, run = _sb_logits(z_new[hh], tri_new, sq, cmask, None, total=True)
        x_old, _ = _sb_logits(z_old[hh], tri2, tile, None, run)
        halves.append(_dot(_sb_weights(x_new, cmask), vs_ref[0][:, sl])
                      + _nt_dot(_sb_weights(x_old), csv_ref[0, 0, hh].astype(BF16)))
    o = jnp.concatenate(halves, axis=-1)
    os_ref[0] = _pair_rms(o, lane, nw_ref[...]).astype(BF16)


def _sample_attn(lam, proj, cdk, cdv, csk, csv, bias_sp, bias_sn, subw, nw2, tri, n_dh):
    b, sq, _ = proj.shape
    past = cdk.shape[3]
    tile = ATT_TILE
    blk = lambda off: pl.BlockSpec((1, sq, LANES), lambda i, h: (i, 0, off + h))
    cd = pl.BlockSpec((1, 1, 1, past, LANES), lambda i, h: (0, i, h, 0, 0))
    cs = pl.BlockSpec((1, 1, 2, HEAD_DIM, past), lambda i, h: (0, i, h, 0, 0))
    const2 = lambda a: pl.BlockSpec(a.shape, lambda i, h: (0, 0))
    out = pl.BlockSpec((1, sq, LANES), lambda i, h: (i, 0, h))
    return pl.pallas_call(
        functools.partial(_sample_attn_kernel, sq=sq, past=past, tile=tile),
        grid=(b, n_dh),
        in_specs=[pl.BlockSpec(memory_space=pltpu.SMEM),
                  blk(0), blk(n_dh), blk(2 * n_dh), blk(3 * n_dh), blk(4 * n_dh), blk(5 * n_dh),
                  cd, cd, cs, cs,
                  pl.BlockSpec((1, sq, past), lambda i, h: (h, 0, 0)),
                  pl.BlockSpec((1, sq, sq), lambda i, h: (h, 0, 0)),
                  const2(subw), const2(nw2), const2(tri)],
        out_specs=[out, out],
        out_shape=[jax.ShapeDtypeStruct((b, sq, n_dh * LANES), BF16)] * 2,
        compiler_params=pltpu.CompilerParams(vmem_limit_bytes=VMEM_LIMIT),
        name="sample_attn",
    )(lam, proj, proj, proj, proj, proj, proj, cdk, cdv, csk, csv, bias_sp, bias_sn, subw, nw2, tri)


CARRY_ROWS = 8


def _mlp_kernel(x_ref, od_ref, os_ref, cinit_ref, wod_ref, wos_ref, fnw_ref, wup_ref, cw_ref, cb_ref,
                wdn_ref, finw_ref, y_ref, nconv_ref, carry_ref, *, ts, ff, fc):
    @pl.when(pl.program_id(1) == 0)
    def _():
        carry_ref[...] = cinit_ref[0]

    x1 = x_ref[0] + _dot(od_ref[0], wod_ref[...]) + _dot(os_ref[0], wos_ref[...])
    h2 = _rms(x1, fnw_ref[...]).astype(BF16)
    groups = ts // CARRY_ROWS
    sub = lax.broadcasted_iota(jnp.int32, (groups, CARRY_ROWS, fc), 1)
    nc = ff // fc

    def up(c):
        return tuple(_dot(h2, wup_ref[:, c0:c0 + fc]) for c0 in (c * fc, ff + c * fc))

    def conv(u, c0):
        cols = slice(c0, c0 + fc)
        u3 = u.reshape(groups, CARRY_ROWS, fc)
        before = carry_ref[:, cols]

        def shifted(k):
            rot = pltpu.roll(u3, k, 1)
            above = pltpu.roll(before, k, 0)[None]
            if groups > 1:
                above = jnp.concatenate([above, rot[:-1]], axis=0)
            return jnp.where(sub < k, above, rot).reshape(ts, fc)

        out = (cb_ref[:, cols] + shifted(2) * cw_ref[0:1, cols] + shifted(1) * cw_ref[1:2, cols]
               + u * cw_ref[2:3, cols])
        tail = u[ts - CARRY_ROWS:, :]
        carry_ref[:, cols] = tail
        nconv_ref[0, :, cols] = tail
        return out

    acc = None
    u = up(0)
    acts = []

    def down(c_end):
        act = jnp.concatenate(acts, axis=1) if len(acts) > 1 else acts[0]
        part = _dot(act, wdn_ref[(c_end - len(acts)) * fc:c_end * fc, :])
        return part if acc is None else acc + part

    for c in range(nc):
        u_next = up(c + 1) if c + 1 < nc else None
        if len(acts) == 2:
            acc = down(c)
            acts = []
        g = conv(u[0], c * fc)
        v = conv(u[1], ff + c * fc)
        acts.append((g / (1.0 + jnp.exp(-g)) * v).astype(BF16))
        u = u_next
    y_ref[0] = _rms(x1 + down(nc), finw_ref[...])


def _mlp(x, od, osb, cinit, wod, wos, fnw, wup, cw, cb, wdn, finw, ts):
    b, s, d = x.shape
    ff = wdn.shape[0]
    half = od.shape[2]
    fc = 256
    const = lambda a: pl.BlockSpec(a.shape, lambda i, j: (0, 0))
    tok = lambda w: pl.BlockSpec((1, ts, w), lambda i, j: (i, j, 0))
    state = pl.BlockSpec((1, CARRY_ROWS, 2 * ff), lambda i, j: (i, 0, 0))
    return pl.pallas_call(
        functools.partial(_mlp_kernel, ts=ts, ff=ff, fc=fc),
        grid=(b, s // ts),
        in_specs=[tok(d), tok(half), tok(half), state, const(wod), const(wos), const(fnw), const(wup),
                  const(cw), const(cb), const(wdn), const(finw)],
        out_specs=[tok(d), state],
        out_shape=[jax.ShapeDtypeStruct((b, s, d), F32),
                   jax.ShapeDtypeStruct((b, CARRY_ROWS, 2 * ff), F32)],
        scratch_shapes=[pltpu.VMEM((CARRY_ROWS, 2 * ff), F32)],
        compiler_params=pltpu.CompilerParams(dimension_semantics=("arbitrary", "arbitrary"),
                                             vmem_limit_bytes=VMEM_LIMIT),
        name="mlp",
    )(x, od, osb, cinit, wod, wos, fnw, wup, cw, cb, wdn, finw)


def kernel(x_prompt, x_sample, cache_diff_k, cache_diff_v, cache_sb_k, cache_sb_v, state_conv, attn_norm_w, w_in, lambda_q1, lambda_k1, lambda_q2, lambda_k2, diff_subln_w, sb_norm_w, w_out, ffn_norm_w, w_up, conv_w, conv_b, w_down, rel_bias, final_norm_w):
    depth = w_in.shape[0]
    assert depth == 1, "single-layer step"
    bp, sp, d = x_prompt.shape
    bs, ss, _ = x_sample.shape
    past = cache_diff_k.shape[3]
    n_dh = cache_diff_k.shape[2]
    n_sh = cache_sb_k.shape[2]
    ff = w_down.shape[1]
    tile = ATT_TILE
    assert sp % tile == 0 and past % tile == 0 and tile % CHUNK == 0 and n_sh == 2 * n_dh
    assert ss % CARRY_ROWS == 0
    lam_init = 0.8 - 0.6 * math.exp(-0.3 * 0)

    loc = np.arange(tile)
    bkt_p = np.stack([_bucket_tiles(loc + tile, loc + tile), _bucket_tiles(loc + tile, loc)])
    far_bucket = int(_rel_bucket_np(np.array([-(tile + 1)]))[0])
    assert far_bucket == int(_rel_bucket_np(np.array([-(sp + past)]))[0])
    qpos_s = past + np.arange(ss)
    bkt_sp = _bucket_tiles(qpos_s, np.arange(past))
    bkt_sn = _bucket_tiles(qpos_s, qpos_s)

    row = lambda v: v.reshape(1, -1).astype(F32)
    bias_p, bias_sp, bias_sn, lam = _prep(
        rel_bias.astype(F32), row(lambda_q1[0]), row(lambda_k1[0]), row(lambda_q2[0]), row(lambda_k2[0]),
        jnp.asarray(bkt_p), jnp.asarray(bkt_sp), jnp.asarray(bkt_sn), lam_init, far_bucket)

    w_in_bf = w_in[0].astype(BF16)
    w_out_bf = w_out[0].astype(BF16)
    half = n_dh * LANES
    wod, wos = w_out_bf[:half], w_out_bf[half:]
    wup = w_up[0].astype(BF16)
    wdn = w_down[0].astype(BF16)
    anw, fnw, finw = row(attn_norm_w[0]), row(ffn_norm_w[0]), row(final_norm_w)
    subw = row(diff_subln_w[0]) * (1.0 - lam_init)
    nw2 = jnp.tile(row(sb_norm_w[0]), (1, 2))
    cw, cb = conv_w[0].astype(F32), row(conv_b[0])
    tri = np.triu(np.ones((tile, tile), np.float32), 1).T
    tri = jnp.asarray(np.concatenate([tri, tri], axis=0), BF16)

    def state8(st):
        return jnp.pad(st, ((0, 0), (CARRY_ROWS - st.shape[1], 0), (0, 0)))

    ts_p = min(512, sp)
    proj_p, p_dk, p_dv, p_sk, p_sv = _inproj(x_prompt, anw, w_in_bf, ts_p)
    od_p = _diff_attn(lam, proj_p, bias_p, subw, n_dh)
    os_p = _sb_attn(proj_p, tri, nw2, n_dh, 3 * n_dh)
    y_p, cv_p = _mlp(x_prompt, od_p, os_p, jnp.zeros((bp, CARRY_ROWS, 2 * ff), F32),
                     wod, wos, fnw, wup, cw, cb, wdn, finw, min(1024, sp))

    proj_s, s_dk, s_dv, s_sk, s_sv = _inproj(x_sample, anw, w_in_bf, ss)
    od_s, os_s = _sample_attn(lam, proj_s, cache_diff_k, cache_diff_v,
                              jnp.swapaxes(cache_sb_k, 3, 4), jnp.swapaxes(cache_sb_v, 3, 4),
                              bias_sp, bias_sn, subw, nw2, tri, n_dh)
    y_s, cv_s = _mlp(x_sample, od_s, os_s, state8(state_conv[0]),
                     wod, wos, fnw, wup, cw, cb, wdn, finw, ss)

    tail = lambda cv: cv[None, :, CARRY_ROWS - 2:, :]
    return (y_p, y_s, p_dk, p_dv, p_sk, p_sv, tail(cv_p), s_dk, s_dv, s_sk, s_sv, tail(cv_s))
```

```python
import functools
import math

import numpy as np
import jax
import jax.numpy as jnp
from jax import lax
from jax.experimental import pallas as pl
from jax.experimental.pallas import tpu as pltpu

F32 = jnp.float32
BF16 = jnp.bfloat16

HEAD_DIM = 64
LANES = 128
CHUNK = 64
N_BUCKETS = 32
EPS = 1e-6
NEG = -1e30
LOG2E = 1.4426950408889634
SIGN_BIT = np.int32(-2 ** 31)
SB_ZERO_LOG2 = -160.0
ATT_TILE = 256
SB_SUB = ATT_TILE // 2
VMEM_LIMIT = 60 * 1024 * 1024


def _rel_bucket_np(rel):
    half = N_BUCKETS // 2
    max_exact = half // 2
    n = np.abs(rel)
    thresholds = (12, 16, 23, 32, 46, 64, 91)
    large = max_exact + sum((n >= t).astype(np.int64) for t in thresholds)
    return (np.where(rel > 0, half, 0) + np.where(n < max_exact, n, large)).astype(np.int32)


def _bucket_tiles(qpos, kpos):
    rel = kpos[None, :] - qpos[:, None]
    vis = (kpos[None, :] // CHUNK) <= (qpos[:, None] // CHUNK)
    return np.where(vis, _rel_bucket_np(rel), -1).astype(np.int32)


def _nt_dot(a, b):
    return lax.dot_general(a, b, (((1,), (1,)), ((), ())), preferred_element_type=F32)


def _dot(a, b):
    return jnp.dot(a, b, preferred_element_type=F32)


def _rms(x, w):
    return x * lax.rsqrt(jnp.mean(x * x, axis=-1, keepdims=True) + EPS) * w


def _prep_kernel(rb_ref, lq1_ref, lk1_ref, lq2_ref, lk2_ref, bp_ref, bsp_ref, bsn_ref,
                 bias_p_ref, bias_sp_ref, bias_sn_ref, lam_ref, *, lam_init, far_bucket):
    h = pl.program_id(0)
    pairs = ((bp_ref, bias_p_ref), (bsp_ref, bias_sp_ref), (bsn_ref, bias_sn_ref))
    for _, out in pairs:
        out[...] = jnp.full(out.shape, NEG, F32)

    def body(b, carry):
        val = rb_ref[b, h]
        for idx, out in pairs:
            out[0] = jnp.where(idx[...] == b, val, out[0])
        return carry

    lax.fori_loop(0, N_BUCKETS, body, 0)
    bias_p_ref[0] = bias_p_ref[0] - rb_ref[far_bucket, h]
    s1 = jnp.sum(lq1_ref[...] * lk1_ref[...], axis=-1, keepdims=True)
    s2 = jnp.sum(lq2_ref[...] * lk2_ref[...], axis=-1, keepdims=True)
    lam_ref[...] = jnp.exp(s1) - jnp.exp(s2) + lam_init


def _prep(rel_bias, lq1, lk1, lq2, lk2, bkt_p, bkt_sp, bkt_sn, lam_init, far_bucket):
    nh = rel_bias.shape[1]
    full = lambda a: pl.BlockSpec(a.shape, lambda h: (0,) * a.ndim)
    head = lambda a: pl.BlockSpec((1,) + a.shape, lambda h: (h,) + (0,) * a.ndim)
    vec = pl.BlockSpec((1, HEAD_DIM), lambda h: (0, 0))
    return pl.pallas_call(
        functools.partial(_prep_kernel, lam_init=lam_init, far_bucket=far_bucket),
        grid=(nh,),
        in_specs=[pl.BlockSpec(memory_space=pltpu.SMEM), vec, vec, vec, vec,
                  full(bkt_p), full(bkt_sp), full(bkt_sn)],
        out_specs=[head(bkt_p), head(bkt_sp), head(bkt_sn), pl.BlockSpec((1, 1), lambda h: (0, 0))],
        out_shape=[jax.ShapeDtypeStruct((nh,) + bkt_p.shape, F32),
                   jax.ShapeDtypeStruct((nh,) + bkt_sp.shape, F32),
                   jax.ShapeDtypeStruct((nh,) + bkt_sn.shape, F32),
                   jax.ShapeDtypeStruct((1, 1), F32)],
        name="prep",
    )(rel_bias, lq1, lk1, lq2, lk2, bkt_p, bkt_sp, bkt_sn)


def _inproj_kernel(x_ref, nw_ref, w_ref, proj_ref, dk_ref, dv_ref, sk_ref, sv_ref, *, width):
    h = _rms(x_ref[0], nw_ref[...]).astype(BF16)
    n_dh = width // LANES
    n_sh = width // HEAD_DIM
    for g in (4, 5, 1, 2, 3, 0):
        acc = _dot(h, w_ref[:, g * width:(g + 1) * width])
        proj_ref[0, :, g * width:(g + 1) * width] = acc.astype(BF16)
        if g in (1, 2):
            out = dk_ref if g == 1 else dv_ref
            for hh in range(n_dh):
                out[0, 0, hh] = acc[:, hh * LANES:(hh + 1) * LANES]
        if g in (4, 5):
            out = sk_ref if g == 4 else sv_ref
            for hh in range(n_sh):
                out[0, 0, hh] = acc[:, hh * HEAD_DIM:(hh + 1) * HEAD_DIM]


def _inproj(x, nw, w_in_bf, ts):
    b, s, d = x.shape
    cols = w_in_bf.shape[1]
    width = cols // 6
    n_dh, n_sh = width // LANES, width // HEAD_DIM
    kv = lambda nh, hd: pl.BlockSpec((1, 1, nh, ts, hd), lambda i, j: (0, i, 0, j, 0))
    return pl.pallas_call(
        functools.partial(_inproj_kernel, width=width),
        grid=(b, s // ts),
        in_specs=[pl.BlockSpec((1, ts, d), lambda i, j: (i, j, 0)),
                  pl.BlockSpec((1, d), lambda i, j: (0, 0)),
                  pl.BlockSpec((d, cols), lambda i, j: (0, 0))],
        out_specs=[pl.BlockSpec((1, ts, cols), lambda i, j: (i, j, 0)),
                   kv(n_dh, LANES), kv(n_dh, LANES), kv(n_sh, HEAD_DIM), kv(n_sh, HEAD_DIM)],
        out_shape=[jax.ShapeDtypeStruct((b, s, cols), BF16),
                   jax.ShapeDtypeStruct((1, b, n_dh, s, LANES), F32),
                   jax.ShapeDtypeStruct((1, b, n_dh, s, LANES), F32),
                   jax.ShapeDtypeStruct((1, b, n_sh, s, HEAD_DIM), F32),
                   jax.ShapeDtypeStruct((1, b, n_sh, s, HEAD_DIM), F32)],
        compiler_params=pltpu.CompilerParams(vmem_limit_bytes=VMEM_LIMIT),
        name="inproj",
    )(x, nw, w_in_bf)


def _diff_finish(a1, l1, a2, l2, lam, subw):
    o = a1 / l1 - lam * (a2 / l2)
    return _rms(o, subw)


def _pair_rms(o, lane, w):
    o2 = o * o
    tot = jnp.sum(o2, axis=-1, keepdims=True)
    lo = jnp.sum(jnp.where(lane < HEAD_DIM, o2, 0.0), axis=-1, keepdims=True)
    ms = jnp.where(lane < HEAD_DIM, lo, tot - lo) * (1.0 / HEAD_DIM)
    return o * lax.rsqrt(ms + EPS) * w


def _diff_kernel(lam_ref, q_ref, k_ref, v_ref, bias_ref, subw_ref, o_ref, *, tile, nq):
    lam = lam_ref[0, 0]
    lane = lax.broadcasted_iota(jnp.int32, (tile, LANES), 1)
    subw = subw_ref[...]
    rowmax = lambda t: jnp.max(t, axis=-1, keepdims=True)
    rowsum = lambda t: jnp.sum(t, axis=-1, keepdims=True)

    def scores(qi, half):
        q = q_ref[0, qi * tile:(qi + 1) * tile, :] * jnp.asarray(HEAD_DIM ** -0.5, BF16)
        in_half = (lane >= half * HEAD_DIM) & (lane < (half + 1) * HEAD_DIM)
        return _nt_dot(jnp.where(in_half, q, jnp.zeros_like(q)), k_ref[0, :(qi + 1) * tile, :])

    def attend(qi, s):
        parts = [s[:, qi * tile:] + bias_ref[0, 0]]
        if qi >= 1:
            parts.append(s[:, (qi - 1) * tile:qi * tile] + bias_ref[0, 1])
        if qi >= 2:
            parts.append(s[:, :(qi - 1) * tile])
        m = functools.reduce(jnp.maximum, [rowmax(t) for t in parts])
        ps = [jnp.exp(t - m) for t in parts]
        l = functools.reduce(jnp.add, [rowsum(t) for t in ps])
        pb = jnp.concatenate([t.astype(BF16) for t in reversed(ps)], axis=1)
        return _dot(pb, v_ref[0, :(qi + 1) * tile, :]), l

    rows = [(qi, half) for qi in range(nq) for half in range(2)]
    pending, done = {}, {}
    for t in range(len(rows) + 1):
        if t < len(rows):
            pending[rows[t]] = scores(*rows[t])
        if t >= 1:
            qi, half = rows[t - 1]
            done[half] = attend(qi, pending.pop((qi, half)))
            if half == 1:
                (a1, l1), (a2, l2) = done[0], done[1]
                o_ref[0, qi * tile:(qi + 1) * tile, :] = _diff_finish(a1, l1, a2, l2, lam, subw).astype(BF16)


def _diff_attn(lam, proj, bias_p, subw, n_heads):
    b, s, _ = proj.shape
    tile = ATT_TILE
    blk = lambda off: pl.BlockSpec((1, s, LANES), lambda i, h: (i, 0, off + h))
    return pl.pallas_call(
        functools.partial(_diff_kernel, tile=tile, nq=s // tile),
        grid=(b, n_heads),
        in_specs=[pl.BlockSpec(memory_space=pltpu.SMEM),
                  blk(0), blk(n_heads), blk(2 * n_heads),
                  pl.BlockSpec((1, 2, tile, tile), lambda i, h: (h, 0, 0, 0)),
                  pl.BlockSpec((1, LANES), lambda i, h: (0, 0))],
        out_specs=pl.BlockSpec((1, s, LANES), lambda i, h: (i, 0, h)),
        out_shape=jax.ShapeDtypeStruct((b, s, n_heads * LANES), BF16),
        compiler_params=pltpu.CompilerParams(vmem_limit_bytes=VMEM_LIMIT),
        name="diff_attn",
    )(lam, proj, proj, proj, bias_p, subw)


def _sb_logits(z2, tiles, run=None, total=False):
    neg_abs = lax.bitcast_convert_type(lax.bitcast_convert_type(z2, jnp.int32) | SIGN_BIT, F32)
    sp = jnp.log2(1.0 + jnp.exp2(neg_abs))
    lb = jnp.minimum(z2, 0.0) - sp
    l1m = lb - z2
    stops = np.cumsum([0] + [width for width, _, _ in tiles])
    xs = [None] * len(tiles)
    for j in range(len(tiles) - 1, -1, -1):
        width, tri2, mask = tiles[j]
        cols = slice(int(stops[j]), int(stops[j + 1]))
        lj = l1m[:, cols]
        if mask is not None:
            lj = jnp.where(mask, lj, 0.0)
        hi = lj.astype(BF16)
        lo = (lj - hi.astype(F32)).astype(BF16)
        if width % LANES == 0:
            suffix = _dot(jnp.concatenate([hi, lo], axis=1), tri2)
        else:
            suffix = _dot(hi, tri2) + _dot(lo, tri2)
        x = lb[:, cols] + suffix
        xs[j] = x if run is None else x + run
        if j > 0 or total:
            rs = jnp.sum(lj, axis=-1, keepdims=True)
            run = rs if run is None else run + rs
    return xs, run


def _sb_weights(xs, tiles):
    ws = []
    for x, (_, _, mask) in zip(xs, tiles):
        w = jnp.exp2(x)
        ws.append((w if mask is None else jnp.where(mask, w, 0.0)).astype(BF16))
    return jnp.concatenate(ws, axis=1) if len(ws) > 1 else ws[0]


def _strict_causal(n):
    return lax.broadcasted_iota(jnp.int32, (n, n), 1) < lax.broadcasted_iota(jnp.int32, (n, n), 0)


def _sb_kernel(q_ref, k_ref, v_ref, tri_ref, tris_ref, nw_ref, o_ref, *, tile, sub, nq):
    tri, tri_s = tri_ref[...], tris_ref[...]
    cmask, cmask_s = _strict_causal(tile), _strict_causal(sub)
    nw = nw_ref[...]
    scale = jnp.asarray(HEAD_DIM ** -0.5, BF16)

    def head_q(q, half):
        lane = lax.broadcasted_iota(jnp.int32, q.shape, 1)
        in_half = (lane >= half * HEAD_DIM) & (lane < (half + 1) * HEAD_DIM)
        return jnp.where(in_half, q, jnp.zeros_like(q))

    def finish(halves):
        lane = lax.broadcasted_iota(jnp.int32, halves[0].shape, 1)
        o = jnp.where(lane < HEAD_DIM, halves[0], halves[1])
        return _pair_rms(o, lane, nw).astype(BF16)

    def window(j):
        end = (j + 1) * sub
        start = max(end - sub - tile, 0)
        before = end - sub - start
        tiles = [(before, tri if before == tile else tri_s, None)] if before else []
        return start, end, tiles + [(sub, tri_s, cmask_s)]

    rows = [(j, half) for j in range(nq * tile // sub) for half in range(2)]
    z2s, xss, halves = {}, {}, {}
    worst = None
    for t in range(len(rows) + 2):
        if t < len(rows):
            j, half = rows[t]
            start, end, _ = window(j)
            q = q_ref[0, j * sub:(j + 1) * sub, :] * scale
            z2s[rows[t]] = _nt_dot(head_q(q, half), k_ref[0, start:end, :]) * LOG2E
        if 1 <= t <= len(rows):
            j, half = rows[t - 1]
            start, _, tiles = window(j)
            xss[j, half], run = _sb_logits(z2s.pop((j, half)), tiles, None, total=start > 0)
            if start > 0:
                worst = run if worst is None else jnp.maximum(worst, run)
        if t >= 2:
            j, half = rows[t - 2]
            start, end, tiles = window(j)
            halves[half] = _dot(_sb_weights(xss.pop((j, half)), tiles), v_ref[0, start:end, :])
            if half == 1:
                o_ref[0, j * sub:(j + 1) * sub, :] = finish(halves)

    if worst is None:
        return

    @pl.when(jnp.max(worst) >= SB_ZERO_LOG2)
    def _():
        def q_body(qi, carry):
            q0 = pl.multiple_of(qi * tile, tile)
            q = q_ref[0, pl.ds(q0, tile), :] * scale
            diag, plain = [(tile, tri, cmask)], [(tile, tri, None)]
            full = []
            for half in range(2):
                qm = head_q(q, half)
                xs, run = _sb_logits(_nt_dot(qm, k_ref[0, pl.ds(q0, tile), :]) * LOG2E, diag, None, total=True)
                acc = _dot(_sb_weights(xs, diag), v_ref[0, pl.ds(q0, tile), :])

                def kb_body(it, c):
                    acc, run = c
                    k0 = pl.multiple_of((qi - 1 - it) * tile, tile)
                    xs, run = _sb_logits(_nt_dot(qm, k_ref[0, pl.ds(k0, tile), :]) * LOG2E, plain, run, total=True)
                    return acc + _dot(_sb_weights(xs, plain), v_ref[0, pl.ds(k0, tile), :]), run

                acc, _ = lax.fori_loop(0, qi, kb_body, (acc, run))
                full.append(acc)
            o_ref[0, pl.ds(q0, tile), :] = finish(full)
            return carry

        lax.fori_loop(1, nq, q_body, 0)


def _sb_attn(proj, tri, tri_s, nw2, n_pairs, off):
    b, s, _ = proj.shape
    tile = ATT_TILE
    blk = lambda o: pl.BlockSpec((1, s, LANES), lambda i, p: (i, 0, o + p))
    return pl.pallas_call(
        functools.partial(_sb_kernel, tile=tile, sub=SB_SUB, nq=s // tile),
        grid=(b, n_pairs),
        in_specs=[blk(off), blk(off + n_pairs), blk(off + 2 * n_pairs),
                  pl.BlockSpec(tri.shape, lambda i, p: (0, 0)),
                  pl.BlockSpec(tri_s.shape, lambda i, p: (0, 0)),
                  pl.BlockSpec((1, LANES), lambda i, p: (0, 0))],
        out_specs=pl.BlockSpec((1, s, LANES), lambda i, p: (i, 0, p)),
        out_shape=jax.ShapeDtypeStruct((b, s, n_pairs * LANES), BF16),
        compiler_params=pltpu.CompilerParams(vmem_limit_bytes=VMEM_LIMIT),
        name="sb_attn",
    )(proj, proj, proj, tri, tri_s, nw2)


def _sample_attn_kernel(lam_ref, qd_ref, kd_ref, vd_ref, qs_ref, ks_ref, vs_ref,
                        cdk_ref, cdv_ref, csk_ref, csv_ref, bsp_ref, bsn_ref,
                        subw_ref, nw_ref, tri_ref, od_ref, os_ref, *, sq, past, tile):
    lam = lam_ref[0, 0]
    scale = jnp.asarray(HEAD_DIM ** -0.5, BF16)
    lane = lax.broadcasted_iota(jnp.int32, (sq, LANES), 1)

    q = qd_ref[0] * scale
    kc = cdk_ref[0, 0, 0].astype(BF16)
    vc = cdv_ref[0, 0, 0].astype(BF16)
    kn = kd_ref[0]
    vn = vd_ref[0]
    bias_c = bsp_ref[0]
    bias_n = bsn_ref[0]
    outs = []
    for qm in (jnp.where(lane < HEAD_DIM, q, jnp.zeros_like(q)),
               jnp.where(lane >= HEAD_DIM, q, jnp.zeros_like(q))):
        sc = _nt_dot(qm, kc) + bias_c
        sn = _nt_dot(qm, kn) + bias_n
        m = jnp.maximum(jnp.max(sc, axis=-1, keepdims=True), jnp.max(sn, axis=-1, keepdims=True))
        pc = jnp.exp(sc - m)
        pn = jnp.exp(sn - m)
        l = jnp.sum(pc, axis=-1, keepdims=True) + jnp.sum(pn, axis=-1, keepdims=True)
        outs.append((_dot(pc.astype(BF16), vc) + _dot(pn.astype(BF16), vn), l))
    (a1, l1), (a2, l2) = outs
    od_ref[0] = _diff_finish(a1, l1, a2, l2, lam, subw_ref[...]).astype(BF16)

    row = lax.broadcasted_iota(jnp.int32, (sq, sq), 0)
    colv = lax.broadcasted_iota(jnp.int32, (sq, sq), 1)
    cmask = colv < row
    tri2 = tri_ref[...]
    tri_new = tri2[:sq, :sq]
    qs = qs_ref[0] * scale
    heads = [slice(hh * HEAD_DIM, (hh + 1) * HEAD_DIM) for hh in range(2)]
    z_new = [_nt_dot(qs[:, sl], ks_ref[0][:, sl]) * LOG2E for sl in heads]
    z_old = [_dot(qs[:, sl], csk_ref[0, 0, hh].astype(BF16)) * LOG2E for hh, sl in enumerate(heads)]
    halves = []
    for hh, sl in enumerate(heads):
        new, old = [(sq, tri_new, cmask)], [(tile, tri2, None)] * (past // tile)
        x_new, run = _sb_logits(z_new[hh], new, None, total=True)
        x_old, _ = _sb_logits(z_old[hh], old, run)
        halves.append(_dot(_sb_weights(x_new, new), vs_ref[0][:, sl])
                      + _nt_dot(_sb_weights(x_old, old), csv_ref[0, 0, hh].astype(BF16)))
    o = jnp.concatenate(halves, axis=-1)
    os_ref[0] = _pair_rms(o, lane, nw_ref[...]).astype(BF16)


def _sample_attn(lam, proj, cdk, cdv, csk, csv, bias_sp, bias_sn, subw, nw2, tri, n_dh):
    b, sq, _ = proj.shape
    past = cdk.shape[3]
    tile = ATT_TILE
    blk = lambda off: pl.BlockSpec((1, sq, LANES), lambda i, h: (i, 0, off + h))
    cd = pl.BlockSpec((1, 1, 1, past, LANES), lambda i, h: (0, i, h, 0, 0))
    cs = pl.BlockSpec((1, 1, 2, HEAD_DIM, past), lambda i, h: (0, i, h, 0, 0))
    const2 = lambda a: pl.BlockSpec(a.shape, lambda i, h: (0, 0))
    out = pl.BlockSpec((1, sq, LANES), lambda i, h: (i, 0, h))
    return pl.pallas_call(
        functools.partial(_sample_attn_kernel, sq=sq, past=past, tile=tile),
        grid=(b, n_dh),
        in_specs=[pl.BlockSpec(memory_space=pltpu.SMEM),
                  blk(0), blk(n_dh), blk(2 * n_dh), blk(3 * n_dh), blk(4 * n_dh), blk(5 * n_dh),
                  cd, cd, cs, cs,
                  pl.BlockSpec((1, sq, past), lambda i, h: (h, 0, 0)),
                  pl.BlockSpec((1, sq, sq), lambda i, h: (h, 0, 0)),
                  const2(subw), const2(nw2), const2(tri)],
        out_specs=[out, out],
        out_shape=[jax.ShapeDtypeStruct((b, sq, n_dh * LANES), BF16)] * 2,
        compiler_params=pltpu.CompilerParams(vmem_limit_bytes=VMEM_LIMIT),
        name="sample_attn",
    )(lam, proj, proj, proj, proj, proj, proj, cdk, cdv, csk, csv, bias_sp, bias_sn, subw, nw2, tri)


CARRY_ROWS = 8


def _mlp_kernel(x_ref, od_ref, os_ref, cinit_ref, wod_ref, wos_ref, fnw_ref, wup_ref, cw_ref, cb_ref,
                wdn_ref, finw_ref, y_ref, nconv_ref, carry_ref, *, ts, ff, fc):
    @pl.when(pl.program_id(1) == 0)
    def _():
        carry_ref[...] = cinit_ref[0]

    x1 = x_ref[0] + _dot(od_ref[0], wod_ref[...]) + _dot(os_ref[0], wos_ref[...])
    h2 = _rms(x1, fnw_ref[...]).astype(BF16)
    groups = ts // CARRY_ROWS
    sub = lax.broadcasted_iota(jnp.int32, (groups, CARRY_ROWS, fc), 1)
    nc = ff // fc

    def up(c):
        return tuple(_dot(h2, wup_ref[:, c0:c0 + fc]) for c0 in (c * fc, ff + c * fc))

    def conv(u, c0):
        cols = slice(c0, c0 + fc)
        u3 = u.reshape(groups, CARRY_ROWS, fc)
        before = carry_ref[:, cols]

        def shifted(k):
            rot = pltpu.roll(u3, k, 1)
            above = pltpu.roll(before, k, 0)[None]
            if groups > 1:
                above = jnp.concatenate([above, rot[:-1]], axis=0)
            return jnp.where(sub < k, above, rot).reshape(ts, fc)

        out = (cb_ref[:, cols] + shifted(2) * cw_ref[0:1, cols] + shifted(1) * cw_ref[1:2, cols]
               + u * cw_ref[2:3, cols])
        tail = u[ts - CARRY_ROWS:, :]
        carry_ref[:, cols] = tail
        nconv_ref[0, :, cols] = tail
        return out

    acc = None
    u = up(0)
    acts = []

    def down(c_end):
        act = jnp.concatenate(acts, axis=1) if len(acts) > 1 else acts[0]
        part = _dot(act, wdn_ref[(c_end - len(acts)) * fc:c_end * fc, :])
        return part if acc is None else acc + part

    for c in range(nc):
        u_next = up(c + 1) if c + 1 < nc else None
        if len(acts) == 2:
            acc = down(c)
            acts = []
        g = conv(u[0], c * fc)
        v = conv(u[1], ff + c * fc)
        acts.append((g / (1.0 + jnp.exp(-g)) * v).astype(BF16))
        u = u_next
    y_ref[0] = _rms(x1 + down(nc), finw_ref[...])


def _mlp(x, od, osb, cinit, wod, wos, fnw, wup, cw, cb, wdn, finw, ts):
    b, s, d = x.shape
    ff = wdn.shape[0]
    half = od.shape[2]
    fc = 256
    const = lambda a: pl.BlockSpec(a.shape, lambda i, j: (0, 0))
    tok = lambda w: pl.BlockSpec((1, ts, w), lambda i, j: (i, j, 0))
    state = pl.BlockSpec((1, CARRY_ROWS, 2 * ff), lambda i, j: (i, 0, 0))
    return pl.pallas_call(
        functools.partial(_mlp_kernel, ts=ts, ff=ff, fc=fc),
        grid=(b, s // ts),
        in_specs=[tok(d), tok(half), tok(half), state, const(wod), const(wos), const(fnw), const(wup),
                  const(cw), const(cb), const(wdn), const(finw)],
        out_specs=[tok(d), state],
        out_shape=[jax.ShapeDtypeStruct((b, s, d), F32),
                   jax.ShapeDtypeStruct((b, CARRY_ROWS, 2 * ff), F32)],
        scratch_shapes=[pltpu.VMEM((CARRY_ROWS, 2 * ff), F32)],
        compiler_params=pltpu.CompilerParams(dimension_semantics=("arbitrary", "arbitrary"),
                                             vmem_limit_bytes=VMEM_LIMIT),
        name="mlp",
    )(x, od, osb, cinit, wod, wos, fnw, wup, cw, cb, wdn, finw)


def kernel(x_prompt, x_sample, cache_diff_k, cache_diff_v, cache_sb_k, cache_sb_v, state_conv, attn_norm_w, w_in, lambda_q1, lambda_k1, lambda_q2, lambda_k2, diff_subln_w, sb_norm_w, w_out, ffn_norm_w, w_up, conv_w, conv_b, w_down, rel_bias, final_norm_w):
    depth = w_in.shape[0]
    assert depth == 1, "single-layer step"
    bp, sp, d = x_prompt.shape
    bs, ss, _ = x_sample.shape
    past = cache_diff_k.shape[3]
    n_dh = cache_diff_k.shape[2]
    n_sh = cache_sb_k.shape[2]
    ff = w_down.shape[1]
    tile = ATT_TILE
    assert sp % tile == 0 and past % tile == 0 and tile % CHUNK == 0 and n_sh == 2 * n_dh
    assert ss % CARRY_ROWS == 0
    lam_init = 0.8 - 0.6 * math.exp(-0.3 * 0)

    loc = np.arange(tile)
    bkt_p = np.stack([_bucket_tiles(loc + tile, loc + tile), _bucket_tiles(loc + tile, loc)])
    far_bucket = int(_rel_bucket_np(np.array([-(tile + 1)]))[0])
    assert far_bucket == int(_rel_bucket_np(np.array([-(sp + past)]))[0])
    qpos_s = past + np.arange(ss)
    bkt_sp = _bucket_tiles(qpos_s, np.arange(past))
    bkt_sn = _bucket_tiles(qpos_s, qpos_s)

    row = lambda v: v.reshape(1, -1).astype(F32)
    bias_p, bias_sp, bias_sn, lam = _prep(
        rel_bias.astype(F32), row(lambda_q1[0]), row(lambda_k1[0]), row(lambda_q2[0]), row(lambda_k2[0]),
        jnp.asarray(bkt_p), jnp.asarray(bkt_sp), jnp.asarray(bkt_sn), lam_init, far_bucket)

    w_in_bf = w_in[0].astype(BF16)
    w_out_bf = w_out[0].astype(BF16)
    half = n_dh * LANES
    wod, wos = w_out_bf[:half], w_out_bf[half:]
    wup = w_up[0].astype(BF16)
    wdn = w_down[0].astype(BF16)
    anw, fnw, finw = row(attn_norm_w[0]), row(ffn_norm_w[0]), row(final_norm_w)
    subw = row(diff_subln_w[0]) * (1.0 - lam_init)
    nw2 = jnp.tile(row(sb_norm_w[0]), (1, 2))
    cw, cb = conv_w[0].astype(F32), row(conv_b[0])
    def suffix_matrix(n):
        m = np.triu(np.ones((n, n), np.float32), 1).T
        return jnp.asarray(np.concatenate([m, m], axis=0), BF16)

    tri, tri_s = suffix_matrix(tile), suffix_matrix(SB_SUB)

    def state8(st):
        return jnp.pad(st, ((0, 0), (CARRY_ROWS - st.shape[1], 0), (0, 0)))

    ts_p = min(512, sp)
    proj_p, p_dk, p_dv, p_sk, p_sv = _inproj(x_prompt, anw, w_in_bf, ts_p)
    od_p = _diff_attn(lam, proj_p, bias_p, subw, n_dh)
    os_p = _sb_attn(proj_p, tri, tri_s, nw2, n_dh, 3 * n_dh)
    y_p, cv_p = _mlp(x_prompt, od_p, os_p, jnp.zeros((bp, CARRY_ROWS, 2 * ff), F32),
                     wod, wos, fnw, wup, cw, cb, wdn, finw, min(1024, sp))

    proj_s, s_dk, s_dv, s_sk, s_sv = _inproj(x_sample, anw, w_in_bf, ss)
    od_s, os_s = _sample_attn(lam, proj_s, cache_diff_k, cache_diff_v,
                              jnp.swapaxes(cache_sb_k, 3, 4), jnp.swapaxes(cache_sb_v, 3, 4),
                              bias_sp, bias_sn, subw, nw2, tri, n_dh)
    y_s, cv_s = _mlp(x_sample, od_s, os_s, state8(state_conv[0]),
                     wod, wos, fnw, wup, cw, cb, wdn, finw, ss)

    tail = lambda cv: cv[None, :, CARRY_ROWS - 2:, :]
    return (y_p, y_s, p_dk, p_dv, p_sk, p_sv, tail(cv_p), s_dk, s_dv, s_sk, s_sv, tail(cv_s))
```

```python
import functools
import math

import numpy as np
import jax
import jax.numpy as jnp
from jax import lax
from jax.experimental import pallas as pl
from jax.experimental.pallas import tpu as pltpu

F32 = jnp.float32
BF16 = jnp.bfloat16

HEAD_DIM = 64
LANES = 128
CHUNK = 64
N_BUCKETS = 32
EPS = 1e-6
NEG = -1e30
LOG2E = 1.4426950408889634
SIGN_BIT = np.int32(-2 ** 31)
SB_ZERO_LOG2 = -160.0
ATT_TILE = 256
SB_SUB = ATT_TILE // 2
VMEM_LIMIT = 60 * 1024 * 1024


def _rel_bucket_np(rel):
    half = N_BUCKETS // 2
    max_exact = half // 2
    n = np.abs(rel)
    thresholds = (12, 16, 23, 32, 46, 64, 91)
    large = max_exact + sum((n >= t).astype(np.int64) for t in thresholds)
    return (np.where(rel > 0, half, 0) + np.where(n < max_exact, n, large)).astype(np.int32)


def _bucket_tiles(qpos, kpos):
    rel = kpos[None, :] - qpos[:, None]
    vis = (kpos[None, :] // CHUNK) <= (qpos[:, None] // CHUNK)
    return np.where(vis, _rel_bucket_np(rel), -1).astype(np.int32)


def _nt_dot(a, b):
    return lax.dot_general(a, b, (((1,), (1,)), ((), ())), preferred_element_type=F32)


def _dot(a, b):
    return jnp.dot(a, b, preferred_element_type=F32)


def _rms(x, w):
    return x * lax.rsqrt(jnp.mean(x * x, axis=-1, keepdims=True) + EPS) * w


def _prep_kernel(rb_ref, lq1_ref, lk1_ref, lq2_ref, lk2_ref, bp_ref, bsp_ref, bsn_ref,
                 bias_p_ref, bias_sp_ref, bias_sn_ref, lam_ref, *, lam_init, far_bucket):
    h = pl.program_id(0)
    pairs = ((bp_ref, bias_p_ref), (bsp_ref, bias_sp_ref), (bsn_ref, bias_sn_ref))
    for _, out in pairs:
        out[...] = jnp.full(out.shape, NEG, F32)

    def body(b, carry):
        val = rb_ref[b, h]
        for idx, out in pairs:
            out[0] = jnp.where(idx[...] == b, val, out[0])
        return carry

    lax.fori_loop(0, N_BUCKETS, body, 0)
    bias_p_ref[0] = bias_p_ref[0] - rb_ref[far_bucket, h]
    s1 = jnp.sum(lq1_ref[...] * lk1_ref[...], axis=-1, keepdims=True)
    s2 = jnp.sum(lq2_ref[...] * lk2_ref[...], axis=-1, keepdims=True)
    lam_ref[...] = jnp.exp(s1) - jnp.exp(s2) + lam_init


def _prep(rel_bias, lq1, lk1, lq2, lk2, bkt_p, bkt_sp, bkt_sn, lam_init, far_bucket):
    nh = rel_bias.shape[1]
    full = lambda a: pl.BlockSpec(a.shape, lambda h: (0,) * a.ndim)
    head = lambda a: pl.BlockSpec((1,) + a.shape, lambda h: (h,) + (0,) * a.ndim)
    vec = pl.BlockSpec((1, HEAD_DIM), lambda h: (0, 0))
    return pl.pallas_call(
        functools.partial(_prep_kernel, lam_init=lam_init, far_bucket=far_bucket),
        grid=(nh,),
        in_specs=[pl.BlockSpec(memory_space=pltpu.SMEM), vec, vec, vec, vec,
                  full(bkt_p), full(bkt_sp), full(bkt_sn)],
        out_specs=[head(bkt_p), head(bkt_sp), head(bkt_sn), pl.BlockSpec((1, 1), lambda h: (0, 0))],
        out_shape=[jax.ShapeDtypeStruct((nh,) + bkt_p.shape, F32),
                   jax.ShapeDtypeStruct((nh,) + bkt_sp.shape, F32),
                   jax.ShapeDtypeStruct((nh,) + bkt_sn.shape, F32),
                   jax.ShapeDtypeStruct((1, 1), F32)],
        name="prep",
    )(rel_bias, lq1, lk1, lq2, lk2, bkt_p, bkt_sp, bkt_sn)


def _inproj_kernel(x_ref, nw_ref, w_ref, proj_ref, dk_ref, dv_ref, sk_ref, sv_ref, *, width, seqs):
    h = _rms(x_ref[0], nw_ref[...]).astype(BF16)
    rows = h.shape[0] // seqs
    for g in (4, 5, 1, 2, 3, 0):
        acc = _dot(h, w_ref[:, g * width:(g + 1) * width])
        proj_ref[0, :, g * width:(g + 1) * width] = acc.astype(BF16)
        out, hd = {1: (dk_ref, LANES), 2: (dv_ref, LANES), 4: (sk_ref, HEAD_DIM), 5: (sv_ref, HEAD_DIM)}.get(
            g, (None, None))
        if out is not None:
            for q in range(seqs):
                for hh in range(width // hd):
                    out[0, q, hh] = acc[q * rows:(q + 1) * rows, hh * hd:(hh + 1) * hd]


def _inproj(x, nw, w_in_bf, ts, seqs=1):
    b, s, d = x.shape
    assert seqs == 1 or ts == seqs * s
    cols = w_in_bf.shape[1]
    width = cols // 6
    n_dh, n_sh = width // LANES, width // HEAD_DIM
    xg = x.reshape(b // seqs, seqs * s, d)
    kv = lambda nh, hd: pl.BlockSpec((1, seqs, nh, ts // seqs, hd), lambda i, j: (0, i, 0, j, 0))
    outs = pl.pallas_call(
        functools.partial(_inproj_kernel, width=width, seqs=seqs),
        grid=(b // seqs, seqs * s // ts),
        in_specs=[pl.BlockSpec((1, ts, d), lambda i, j: (i, j, 0)),
                  pl.BlockSpec((1, d), lambda i, j: (0, 0)),
                  pl.BlockSpec((d, cols), lambda i, j: (0, 0))],
        out_specs=[pl.BlockSpec((1, ts, cols), lambda i, j: (i, j, 0)),
                   kv(n_dh, LANES), kv(n_dh, LANES), kv(n_sh, HEAD_DIM), kv(n_sh, HEAD_DIM)],
        out_shape=[jax.ShapeDtypeStruct(xg.shape[:2] + (cols,), BF16),
                   jax.ShapeDtypeStruct((1, b, n_dh, s, LANES), F32),
                   jax.ShapeDtypeStruct((1, b, n_dh, s, LANES), F32),
                   jax.ShapeDtypeStruct((1, b, n_sh, s, HEAD_DIM), F32),
                   jax.ShapeDtypeStruct((1, b, n_sh, s, HEAD_DIM), F32)],
        compiler_params=pltpu.CompilerParams(vmem_limit_bytes=VMEM_LIMIT),
        name="inproj",
    )(xg, nw, w_in_bf)
    return (outs[0].reshape(b, s, cols),) + tuple(outs[1:])


def _diff_finish(a1, l1, a2, l2, lam, subw):
    o = a1 / l1 - lam * (a2 / l2)
    return _rms(o, subw)


def _pair_rms(o, lane, w):
    o2 = o * o
    tot = jnp.sum(o2, axis=-1, keepdims=True)
    lo = jnp.sum(jnp.where(lane < HEAD_DIM, o2, 0.0), axis=-1, keepdims=True)
    ms = jnp.where(lane < HEAD_DIM, lo, tot - lo) * (1.0 / HEAD_DIM)
    return o * lax.rsqrt(ms + EPS) * w


def _diff_kernel(lam_ref, q_ref, k_ref, v_ref, bias_ref, subw_ref, o_ref, *, tile, nq):
    lam = lam_ref[0, 0]
    lane = lax.broadcasted_iota(jnp.int32, (tile, LANES), 1)
    subw = subw_ref[...]
    rowmax = lambda t: jnp.max(t, axis=-1, keepdims=True)
    rowsum = lambda t: jnp.sum(t, axis=-1, keepdims=True)

    def scores(qi, half):
        q = q_ref[0, qi * tile:(qi + 1) * tile, :] * jnp.asarray(HEAD_DIM ** -0.5, BF16)
        in_half = (lane >= half * HEAD_DIM) & (lane < (half + 1) * HEAD_DIM)
        return _nt_dot(jnp.where(in_half, q, jnp.zeros_like(q)), k_ref[0, :(qi + 1) * tile, :])

    def attend(qi, s):
        parts = [s[:, qi * tile:] + bias_ref[0, 0]]
        if qi >= 1:
            parts.append(s[:, (qi - 1) * tile:qi * tile] + bias_ref[0, 1])
        if qi >= 2:
            parts.append(s[:, :(qi - 1) * tile])
        m = functools.reduce(jnp.maximum, [rowmax(t) for t in parts])
        ps = [jnp.exp(t - m) for t in parts]
        l = functools.reduce(jnp.add, [rowsum(t) for t in ps])
        pb = jnp.concatenate([t.astype(BF16) for t in reversed(ps)], axis=1)
        return _dot(pb, v_ref[0, :(qi + 1) * tile, :]), l

    rows = [(qi, half) for qi in range(nq) for half in range(2)]
    pending, done = {}, {}
    for t in range(len(rows) + 1):
        if t < len(rows):
            pending[rows[t]] = scores(*rows[t])
        if t >= 1:
            qi, half = rows[t - 1]
            done[half] = attend(qi, pending.pop((qi, half)))
            if half == 1:
                (a1, l1), (a2, l2) = done[0], done[1]
                o_ref[0, qi * tile:(qi + 1) * tile, :] = _diff_finish(a1, l1, a2, l2, lam, subw).astype(BF16)


def _diff_attn(lam, proj, bias_p, subw, n_heads):
    b, s, _ = proj.shape
    tile = ATT_TILE
    blk = lambda off: pl.BlockSpec((1, s, LANES), lambda i, h: (i, 0, off + h))
    return pl.pallas_call(
        functools.partial(_diff_kernel, tile=tile, nq=s // tile),
        grid=(b, n_heads),
        in_specs=[pl.BlockSpec(memory_space=pltpu.SMEM),
                  blk(0), blk(n_heads), blk(2 * n_heads),
                  pl.BlockSpec((1, 2, tile, tile), lambda i, h: (h, 0, 0, 0)),
                  pl.BlockSpec((1, LANES), lambda i, h: (0, 0))],
        out_specs=pl.BlockSpec((1, s, LANES), lambda i, h: (i, 0, h)),
        out_shape=jax.ShapeDtypeStruct((b, s, n_heads * LANES), BF16),
        compiler_params=pltpu.CompilerParams(vmem_limit_bytes=VMEM_LIMIT),
        name="diff_attn",
    )(lam, proj, proj, proj, bias_p, subw)


def _sb_logits(z2, tiles, run=None, total=False):
    neg_abs = lax.bitcast_convert_type(lax.bitcast_convert_type(z2, jnp.int32) | SIGN_BIT, F32)
    sp = jnp.log2(1.0 + jnp.exp2(neg_abs))
    lb = jnp.minimum(z2, 0.0) - sp
    l1m = lb - z2
    stops = np.cumsum([0] + [width for width, _, _ in tiles])
    xs = [None] * len(tiles)
    for j in range(len(tiles) - 1, -1, -1):
        width, tri2, mask = tiles[j]
        cols = slice(int(stops[j]), int(stops[j + 1]))
        lj = l1m[:, cols]
        if mask is not None:
            lj = jnp.where(mask, lj, 0.0)
        hi = lj.astype(BF16)
        lo = (lj - hi.astype(F32)).astype(BF16)
        if width % LANES == 0:
            suffix = _dot(jnp.concatenate([hi, lo], axis=1), tri2)
        else:
            suffix = _dot(hi, tri2) + _dot(lo, tri2)
        x = lb[:, cols] + suffix
        xs[j] = x if run is None else x + run
        if j > 0 or total:
            rs = jnp.sum(lj, axis=-1, keepdims=True)
            run = rs if run is None else run + rs
    return xs, run


def _sb_weights(xs, tiles):
    ws = []
    for x, (_, _, mask) in zip(xs, tiles):
        w = jnp.exp2(x)
        ws.append((w if mask is None else jnp.where(mask, w, 0.0)).astype(BF16))
    return jnp.concatenate(ws, axis=1) if len(ws) > 1 else ws[0]


def _strict_causal(n):
    return lax.broadcasted_iota(jnp.int32, (n, n), 1) < lax.broadcasted_iota(jnp.int32, (n, n), 0)


def _sb_kernel(q_ref, k_ref, v_ref, tri_ref, tris_ref, nw_ref, o_ref, *, tile, sub, nq):
    tri, tri_s = tri_ref[...], tris_ref[...]
    cmask, cmask_s = _strict_causal(tile), _strict_causal(sub)
    nw = nw_ref[...]
    scale = jnp.asarray(HEAD_DIM ** -0.5, BF16)

    def head_q(q, half):
        lane = lax.broadcasted_iota(jnp.int32, q.shape, 1)
        in_half = (lane >= half * HEAD_DIM) & (lane < (half + 1) * HEAD_DIM)
        return jnp.where(in_half, q, jnp.zeros_like(q))

    def finish(halves):
        lane = lax.broadcasted_iota(jnp.int32, halves[0].shape, 1)
        o = jnp.where(lane < HEAD_DIM, halves[0], halves[1])
        return _pair_rms(o, lane, nw).astype(BF16)

    def window(j):
        end = (j + 1) * sub
        start = max(end - sub - tile, 0)
        before = end - sub - start
        tiles = [(before, tri if before == tile else tri_s, None)] if before else []
        return start, end, tiles + [(sub, tri_s, cmask_s)]

    rows = [(j, half) for j in range(nq * tile // sub) for half in range(2)]
    z2s, xss, halves = {}, {}, {}
    worst = None
    for t in range(len(rows) + 2):
        if t < len(rows):
            j, half = rows[t]
            start, end, _ = window(j)
            q = q_ref[0, j * sub:(j + 1) * sub, :] * scale
            z2s[rows[t]] = _nt_dot(head_q(q, half), k_ref[0, start:end, :]) * LOG2E
        if 1 <= t <= len(rows):
            j, half = rows[t - 1]
            start, _, tiles = window(j)
            xss[j, half], run = _sb_logits(z2s.pop((j, half)), tiles, None, total=start > 0)
            if start > 0:
                worst = run if worst is None else jnp.maximum(worst, run)
        if t >= 2:
            j, half = rows[t - 2]
            start, end, tiles = window(j)
            halves[half] = _dot(_sb_weights(xss.pop((j, half)), tiles), v_ref[0, start:end, :])
            if half == 1:
                o_ref[0, j * sub:(j + 1) * sub, :] = finish(halves)

    if worst is None:
        return

    @pl.when(jnp.max(worst) >= SB_ZERO_LOG2)
    def _():
        def q_body(qi, carry):
            q0 = pl.multiple_of(qi * tile, tile)
            q = q_ref[0, pl.ds(q0, tile), :] * scale
            diag, plain = [(tile, tri, cmask)], [(tile, tri, None)]
            full = []
            for half in range(2):
                qm = head_q(q, half)
                xs, run = _sb_logits(_nt_dot(qm, k_ref[0, pl.ds(q0, tile), :]) * LOG2E, diag, None, total=True)
                acc = _dot(_sb_weights(xs, diag), v_ref[0, pl.ds(q0, tile), :])

                def kb_body(it, c):
                    acc, run = c
                    k0 = pl.multiple_of((qi - 1 - it) * tile, tile)
                    xs, run = _sb_logits(_nt_dot(qm, k_ref[0, pl.ds(k0, tile), :]) * LOG2E, plain, run, total=True)
                    return acc + _dot(_sb_weights(xs, plain), v_ref[0, pl.ds(k0, tile), :]), run

                acc, _ = lax.fori_loop(0, qi, kb_body, (acc, run))
                full.append(acc)
            o_ref[0, pl.ds(q0, tile), :] = finish(full)
            return carry

        lax.fori_loop(1, nq, q_body, 0)


def _sb_attn(proj, tri, tri_s, nw2, n_pairs, off):
    b, s, _ = proj.shape
    tile = ATT_TILE
    blk = lambda o: pl.BlockSpec((1, s, LANES), lambda i, p: (i, 0, o + p))
    return pl.pallas_call(
        functools.partial(_sb_kernel, tile=tile, sub=SB_SUB, nq=s // tile),
        grid=(b, n_pairs),
        in_specs=[blk(off), blk(off + n_pairs), blk(off + 2 * n_pairs),
                  pl.BlockSpec(tri.shape, lambda i, p: (0, 0)),
                  pl.BlockSpec(tri_s.shape, lambda i, p: (0, 0)),
                  pl.BlockSpec((1, LANES), lambda i, p: (0, 0))],
        out_specs=pl.BlockSpec((1, s, LANES), lambda i, p: (i, 0, p)),
        out_shape=jax.ShapeDtypeStruct((b, s, n_pairs * LANES), BF16),
        compiler_params=pltpu.CompilerParams(vmem_limit_bytes=VMEM_LIMIT),
        name="sb_attn",
    )(proj, proj, proj, tri, tri_s, nw2)


def _sample_attn_kernel(lam_ref, qd_ref, kd_ref, vd_ref, qs_ref, ks_ref, vs_ref,
                        cdk_ref, cdv_ref, csk_ref, csv_ref, bsp_ref, bsn_ref,
                        subw_ref, nw_ref, tri_ref, od_ref, os_ref, *, sq, past, tile, heads):
    lam = lam_ref[0, 0]
    scale = jnp.asarray(HEAD_DIM ** -0.5, BF16)
    lane = lax.broadcasted_iota(jnp.int32, (sq, LANES), 1)
    cmask = _strict_causal(sq)
    tri2 = tri_ref[...]
    tri_new = tri2[:sq, :sq]
    halves_of = [slice(hh * HEAD_DIM, (hh + 1) * HEAD_DIM) for hh in range(2)]

    for h in range(heads):
        cols = slice(h * LANES, (h + 1) * LANES)

        q = qd_ref[0][:, cols] * scale
        kc = cdk_ref[0, 0, h].astype(BF16)
        vc = cdv_ref[0, 0, h].astype(BF16)
        kn = kd_ref[0][:, cols]
        vn = vd_ref[0][:, cols]
        outs = []
        for qm in (jnp.where(lane < HEAD_DIM, q, jnp.zeros_like(q)),
                   jnp.where(lane >= HEAD_DIM, q, jnp.zeros_like(q))):
            sc = _nt_dot(qm, kc) + bsp_ref[h]
            sn = _nt_dot(qm, kn) + bsn_ref[h]
            m = jnp.maximum(jnp.max(sc, axis=-1, keepdims=True), jnp.max(sn, axis=-1, keepdims=True))
            pc = jnp.exp(sc - m)
            pn = jnp.exp(sn - m)
            l = jnp.sum(pc, axis=-1, keepdims=True) + jnp.sum(pn, axis=-1, keepdims=True)
            outs.append((_dot(pc.astype(BF16), vc) + _dot(pn.astype(BF16), vn), l))
        (a1, l1), (a2, l2) = outs
        od_ref[0, :, cols] = _diff_finish(a1, l1, a2, l2, lam, subw_ref[...]).astype(BF16)

        qs = qs_ref[0][:, cols] * scale
        ks = ks_ref[0][:, cols]
        vs = vs_ref[0][:, cols]
        z_new = [_nt_dot(qs[:, sl], ks[:, sl]) * LOG2E for sl in halves_of]
        z_old = [_dot(qs[:, sl], csk_ref[0, 0, 2 * h + hh].astype(BF16)) * LOG2E
                 for hh, sl in enumerate(halves_of)]
        halves = []
        for hh, sl in enumerate(halves_of):
            new, old = [(sq, tri_new, cmask)], [(tile, tri2, None)] * (past // tile)
            x_new, run = _sb_logits(z_new[hh], new, None, total=True)
            x_old, _ = _sb_logits(z_old[hh], old, run)
            halves.append(_dot(_sb_weights(x_new, new), vs[:, sl])
                          + _nt_dot(_sb_weights(x_old, old), csv_ref[0, 0, 2 * h + hh].astype(BF16)))
        o = jnp.concatenate(halves, axis=-1)
        os_ref[0, :, cols] = _pair_rms(o, lane, nw_ref[...]).astype(BF16)


def _sample_attn(lam, proj, cdk, cdv, csk, csv, bias_sp, bias_sn, subw, nw2, tri, n_dh):
    b, sq, _ = proj.shape
    past = cdk.shape[3]
    tile = ATT_TILE
    wide = n_dh * LANES
    blk = lambda off: pl.BlockSpec((1, sq, wide), lambda i: (i, 0, off))
    cd = pl.BlockSpec((1, 1, n_dh, past, LANES), lambda i: (0, i, 0, 0, 0))
    cs = pl.BlockSpec((1, 1, 2 * n_dh, HEAD_DIM, past), lambda i: (0, i, 0, 0, 0))
    full = lambda a: pl.BlockSpec(a.shape, lambda i: (0,) * a.ndim)
    out = pl.BlockSpec((1, sq, wide), lambda i: (i, 0, 0))
    return pl.pallas_call(
        functools.partial(_sample_attn_kernel, sq=sq, past=past, tile=tile, heads=n_dh),
        grid=(b,),
        in_specs=[pl.BlockSpec(memory_space=pltpu.SMEM),
                  blk(0), blk(1), blk(2), blk(3), blk(4), blk(5),
                  cd, cd, cs, cs, full(bias_sp), full(bias_sn), full(subw), full(nw2), full(tri)],
        out_specs=[out, out],
        out_shape=[jax.ShapeDtypeStruct((b, sq, wide), BF16)] * 2,
        compiler_params=pltpu.CompilerParams(vmem_limit_bytes=VMEM_LIMIT),
        name="sample_attn",
    )(lam, proj, proj, proj, proj, proj, proj, cdk, cdv, csk, csv, bias_sp, bias_sn, subw, nw2, tri)


CARRY_ROWS = 8


def _mlp_kernel(x_ref, od_ref, os_ref, cinit_ref, wod_ref, wos_ref, fnw_ref, wup_ref, cw_ref, cb_ref,
                wdn_ref, finw_ref, y_ref, nconv_ref, carry_ref, *, ts, ff, fc):
    @pl.when(pl.program_id(1) == 0)
    def _():
        carry_ref[...] = cinit_ref[0]

    x1 = x_ref[0] + _dot(od_ref[0], wod_ref[...]) + _dot(os_ref[0], wos_ref[...])
    h2 = _rms(x1, fnw_ref[...]).astype(BF16)
    groups = ts // CARRY_ROWS
    sub = lax.broadcasted_iota(jnp.int32, (groups, CARRY_ROWS, fc), 1)
    nc = ff // fc

    def up(c):
        return tuple(_dot(h2, wup_ref[:, c0:c0 + fc]) for c0 in (c * fc, ff + c * fc))

    def conv(u, c0):
        cols = slice(c0, c0 + fc)
        u3 = u.reshape(groups, CARRY_ROWS, fc)
        before = carry_ref[:, cols]

        def shifted(k):
            rot = pltpu.roll(u3, k, 1)
            above = pltpu.roll(before, k, 0)[None]
            if groups > 1:
                above = jnp.concatenate([above, rot[:-1]], axis=0)
            return jnp.where(sub < k, above, rot).reshape(ts, fc)

        out = (cb_ref[:, cols] + shifted(2) * cw_ref[0:1, cols] + shifted(1) * cw_ref[1:2, cols]
               + u * cw_ref[2:3, cols])
        tail = u[ts - CARRY_ROWS:, :]
        carry_ref[:, cols] = tail
        nconv_ref[0, :, cols] = tail
        return out

    acc = None
    u = up(0)
    acts = []

    def down(c_end):
        act = jnp.concatenate(acts, axis=1) if len(acts) > 1 else acts[0]
        part = _dot(act, wdn_ref[(c_end - len(acts)) * fc:c_end * fc, :])
        return part if acc is None else acc + part

    for c in range(nc):
        u_next = up(c + 1) if c + 1 < nc else None
        if len(acts) == 2:
            acc = down(c)
            acts = []
        g = conv(u[0], c * fc)
        v = conv(u[1], ff + c * fc)
        acts.append((g / (1.0 + jnp.exp(-g)) * v).astype(BF16))
        u = u_next
    y_ref[0] = _rms(x1 + down(nc), finw_ref[...])


def _mlp(x, od, osb, cinit, wod, wos, fnw, wup, cw, cb, wdn, finw, ts):
    b, s, d = x.shape
    ff = wdn.shape[0]
    half = od.shape[2]
    fc = 256
    const = lambda a: pl.BlockSpec(a.shape, lambda i, j: (0, 0))
    tok = lambda w: pl.BlockSpec((1, ts, w), lambda i, j: (i, j, 0))
    state = pl.BlockSpec((1, CARRY_ROWS, 2 * ff), lambda i, j: (i, 0, 0))
    return pl.pallas_call(
        functools.partial(_mlp_kernel, ts=ts, ff=ff, fc=fc),
        grid=(b, s // ts),
        in_specs=[tok(d), tok(half), tok(half), state, const(wod), const(wos), const(fnw), const(wup),
                  const(cw), const(cb), const(wdn), const(finw)],
        out_specs=[tok(d), state],
        out_shape=[jax.ShapeDtypeStruct((b, s, d), F32),
                   jax.ShapeDtypeStruct((b, CARRY_ROWS, 2 * ff), F32)],
        scratch_shapes=[pltpu.VMEM((CARRY_ROWS, 2 * ff), F32)],
        compiler_params=pltpu.CompilerParams(dimension_semantics=("arbitrary", "arbitrary"),
                                             vmem_limit_bytes=VMEM_LIMIT),
        name="mlp",
    )(x, od, osb, cinit, wod, wos, fnw, wup, cw, cb, wdn, finw)


def kernel(x_prompt, x_sample, cache_diff_k, cache_diff_v, cache_sb_k, cache_sb_v, state_conv, attn_norm_w, w_in, lambda_q1, lambda_k1, lambda_q2, lambda_k2, diff_subln_w, sb_norm_w, w_out, ffn_norm_w, w_up, conv_w, conv_b, w_down, rel_bias, final_norm_w):
    depth = w_in.shape[0]
    assert depth == 1, "single-layer step"
    bp, sp, d = x_prompt.shape
    bs, ss, _ = x_sample.shape
    past = cache_diff_k.shape[3]
    n_dh = cache_diff_k.shape[2]
    n_sh = cache_sb_k.shape[2]
    ff = w_down.shape[1]
    tile = ATT_TILE
    assert sp % tile == 0 and past % tile == 0 and tile % CHUNK == 0 and n_sh == 2 * n_dh
    assert ss % CARRY_ROWS == 0
    lam_init = 0.8 - 0.6 * math.exp(-0.3 * 0)

    loc = np.arange(tile)
    bkt_p = np.stack([_bucket_tiles(loc + tile, loc + tile), _bucket_tiles(loc + tile, loc)])
    far_bucket = int(_rel_bucket_np(np.array([-(tile + 1)]))[0])
    assert far_bucket == int(_rel_bucket_np(np.array([-(sp + past)]))[0])
    qpos_s = past + np.arange(ss)
    bkt_sp = _bucket_tiles(qpos_s, np.arange(past))
    bkt_sn = _bucket_tiles(qpos_s, qpos_s)

    row = lambda v: v.reshape(1, -1).astype(F32)
    bias_p, bias_sp, bias_sn, lam = _prep(
        rel_bias.astype(F32), row(lambda_q1[0]), row(lambda_k1[0]), row(lambda_q2[0]), row(lambda_k2[0]),
        jnp.asarray(bkt_p), jnp.asarray(bkt_sp), jnp.asarray(bkt_sn), lam_init, far_bucket)

    w_in_bf = w_in[0].astype(BF16)
    w_out_bf = w_out[0].astype(BF16)
    half = n_dh * LANES
    wod, wos = w_out_bf[:half], w_out_bf[half:]
    wup = w_up[0].astype(BF16)
    wdn = w_down[0].astype(BF16)
    anw, fnw, finw = row(attn_norm_w[0]), row(ffn_norm_w[0]), row(final_norm_w)
    subw = row(diff_subln_w[0]) * (1.0 - lam_init)
    nw2 = jnp.tile(row(sb_norm_w[0]), (1, 2))
    cw, cb = conv_w[0].astype(F32), row(conv_b[0])
    def suffix_matrix(n):
        m = np.triu(np.ones((n, n), np.float32), 1).T
        return jnp.asarray(np.concatenate([m, m], axis=0), BF16)

    tri, tri_s = suffix_matrix(tile), suffix_matrix(SB_SUB)

    def state8(st):
        return jnp.pad(st, ((0, 0), (CARRY_ROWS - st.shape[1], 0), (0, 0)))

    ts_p = min(512, sp)
    proj_p, p_dk, p_dv, p_sk, p_sv = _inproj(x_prompt, anw, w_in_bf, ts_p)
    od_p = _diff_attn(lam, proj_p, bias_p, subw, n_dh)
    os_p = _sb_attn(proj_p, tri, tri_s, nw2, n_dh, 3 * n_dh)
    y_p, cv_p = _mlp(x_prompt, od_p, os_p, jnp.zeros((bp, CARRY_ROWS, 2 * ff), F32),
                     wod, wos, fnw, wup, cw, cb, wdn, finw, min(1024, sp))

    proj_s, s_dk, s_dv, s_sk, s_sv = _inproj(x_sample, anw, w_in_bf, bs * ss, seqs=bs)
    od_s, os_s = _sample_attn(lam, proj_s, cache_diff_k, cache_diff_v,
                              jnp.swapaxes(cache_sb_k, 3, 4), jnp.swapaxes(cache_sb_v, 3, 4),
                              bias_sp, bias_sn, subw, nw2, tri, n_dh)
    y_s, cv_s = _mlp(x_sample, od_s, os_s, state8(state_conv[0]),
                     wod, wos, fnw, wup, cw, cb, wdn, finw, ss)

    tail = lambda cv: cv[None, :, CARRY_ROWS - 2:, :]
    return (y_p, y_s, p_dk, p_dv, p_sk, p_sv, tail(cv_p), s_dk, s_dv, s_sk, s_sv, tail(cv_s))
```

```python
import functools
import math

import numpy as np
import jax
import jax.numpy as jnp
from jax import lax
from jax.experimental import pallas as pl
from jax.experimental.pallas import tpu as pltpu

F32 = jnp.float32
BF16 = jnp.bfloat16

HEAD_DIM = 64
LANES = 128
CHUNK = 64
N_BUCKETS = 32
EPS = 1e-6
NEG = -1e30
LOG2E = 1.4426950408889634
SIGN_BIT = np.int32(-2 ** 31)
SB_ZERO_LOG2 = -160.0
ATT_TILE = 256
SB_SUB = ATT_TILE // 2
VMEM_LIMIT = 60 * 1024 * 1024


def _rel_bucket_np(rel):
    half = N_BUCKETS // 2
    max_exact = half // 2
    n = np.abs(rel)
    thresholds = (12, 16, 23, 32, 46, 64, 91)
    large = max_exact + sum((n >= t).astype(np.int64) for t in thresholds)
    return (np.where(rel > 0, half, 0) + np.where(n < max_exact, n, large)).astype(np.int32)


def _bucket_tiles(qpos, kpos):
    rel = kpos[None, :] - qpos[:, None]
    vis = (kpos[None, :] // CHUNK) <= (qpos[:, None] // CHUNK)
    return np.where(vis, _rel_bucket_np(rel), -1).astype(np.int32)


def _nt_dot(a, b):
    return lax.dot_general(a, b, (((1,), (1,)), ((), ())), preferred_element_type=F32)


def _dot(a, b):
    return jnp.dot(a, b, preferred_element_type=F32)


def _rms(x, w):
    return x * lax.rsqrt(jnp.mean(x * x, axis=-1, keepdims=True) + EPS) * w


def _prep_kernel(rb_ref, lq1_ref, lk1_ref, lq2_ref, lk2_ref, bp_ref, bsp_ref, bsn_ref,
                 bias_p_ref, bias_sp_ref, bias_sn_ref, lam_ref, *, lam_init, far_bucket):
    h = pl.program_id(0)
    pairs = ((bp_ref, bias_p_ref), (bsp_ref, bias_sp_ref), (bsn_ref, bias_sn_ref))
    for _, out in pairs:
        out[...] = jnp.full(out.shape, NEG, F32)

    def body(b, carry):
        val = rb_ref[b, h]
        for idx, out in pairs:
            out[0] = jnp.where(idx[...] == b, val, out[0])
        return carry

    lax.fori_loop(0, N_BUCKETS, body, 0)
    bias_p_ref[0] = bias_p_ref[0] - rb_ref[far_bucket, h]
    s1 = jnp.sum(lq1_ref[...] * lk1_ref[...], axis=-1, keepdims=True)
    s2 = jnp.sum(lq2_ref[...] * lk2_ref[...], axis=-1, keepdims=True)
    lam_ref[...] = jnp.exp(s1) - jnp.exp(s2) + lam_init


def _prep(rel_bias, lq1, lk1, lq2, lk2, bkt_p, bkt_sp, bkt_sn, lam_init, far_bucket):
    nh = rel_bias.shape[1]
    full = lambda a: pl.BlockSpec(a.shape, lambda h: (0,) * a.ndim)
    head = lambda a: pl.BlockSpec((1,) + a.shape, lambda h: (h,) + (0,) * a.ndim)
    vec = pl.BlockSpec((1, HEAD_DIM), lambda h: (0, 0))
    return pl.pallas_call(
        functools.partial(_prep_kernel, lam_init=lam_init, far_bucket=far_bucket),
        grid=(nh,),
        in_specs=[pl.BlockSpec(memory_space=pltpu.SMEM), vec, vec, vec, vec,
                  full(bkt_p), full(bkt_sp), full(bkt_sn)],
        out_specs=[head(bkt_p), head(bkt_sp), head(bkt_sn), pl.BlockSpec((1, 1), lambda h: (0, 0))],
        out_shape=[jax.ShapeDtypeStruct((nh,) + bkt_p.shape, F32),
                   jax.ShapeDtypeStruct((nh,) + bkt_sp.shape, F32),
                   jax.ShapeDtypeStruct((nh,) + bkt_sn.shape, F32),
                   jax.ShapeDtypeStruct((1, 1), F32)],
        name="prep",
    )(rel_bias, lq1, lk1, lq2, lk2, bkt_p, bkt_sp, bkt_sn)


def _inproj_kernel(x_ref, nw_ref, w_ref, proj_ref, dk_ref, dv_ref, sk_ref, sv_ref, *, width, seqs):
    h = _rms(x_ref[0], nw_ref[...]).astype(BF16)
    rows = h.shape[0] // seqs
    for g in (4, 5, 1, 2, 3, 0):
        acc = _dot(h, w_ref[:, g * width:(g + 1) * width])
        proj_ref[0, :, g * width:(g + 1) * width] = acc.astype(BF16)
        out, hd = {1: (dk_ref, LANES), 2: (dv_ref, LANES), 4: (sk_ref, HEAD_DIM), 5: (sv_ref, HEAD_DIM)}.get(
            g, (None, None))
        if out is not None:
            for q in range(seqs):
                for hh in range(width // hd):
                    out[0, q, hh] = acc[q * rows:(q + 1) * rows, hh * hd:(hh + 1) * hd]


def _inproj(x, nw, w_in_bf, ts, seqs=1):
    b, s, d = x.shape
    assert seqs == 1 or ts == seqs * s
    cols = w_in_bf.shape[1]
    width = cols // 6
    n_dh, n_sh = width // LANES, width // HEAD_DIM
    xg = x.reshape(b // seqs, seqs * s, d)
    kv = lambda nh, hd: pl.BlockSpec((1, seqs, nh, ts // seqs, hd), lambda i, j: (0, i, 0, j, 0))
    outs = pl.pallas_call(
        functools.partial(_inproj_kernel, width=width, seqs=seqs),
        grid=(b // seqs, seqs * s // ts),
        in_specs=[pl.BlockSpec((1, ts, d), lambda i, j: (i, j, 0)),
                  pl.BlockSpec((1, d), lambda i, j: (0, 0)),
                  pl.BlockSpec((d, cols), lambda i, j: (0, 0))],
        out_specs=[pl.BlockSpec((1, ts, cols), lambda i, j: (i, j, 0)),
                   kv(n_dh, LANES), kv(n_dh, LANES), kv(n_sh, HEAD_DIM), kv(n_sh, HEAD_DIM)],
        out_shape=[jax.ShapeDtypeStruct(xg.shape[:2] + (cols,), BF16),
                   jax.ShapeDtypeStruct((1, b, n_dh, s, LANES), F32),
                   jax.ShapeDtypeStruct((1, b, n_dh, s, LANES), F32),
                   jax.ShapeDtypeStruct((1, b, n_sh, s, HEAD_DIM), F32),
                   jax.ShapeDtypeStruct((1, b, n_sh, s, HEAD_DIM), F32)],
        compiler_params=pltpu.CompilerParams(vmem_limit_bytes=VMEM_LIMIT),
        name="inproj",
    )(xg, nw, w_in_bf)
    return (outs[0].reshape(b, s, cols),) + tuple(outs[1:])


def _diff_finish(a1, l1, a2, l2, lam, subw):
    o = a1 / l1 - lam * (a2 / l2)
    return _rms(o, subw)


def _pair_rms(o, lane, w):
    o2 = o * o
    tot = jnp.sum(o2, axis=-1, keepdims=True)
    lo = jnp.sum(jnp.where(lane < HEAD_DIM, o2, 0.0), axis=-1, keepdims=True)
    ms = jnp.where(lane < HEAD_DIM, lo, tot - lo) * (1.0 / HEAD_DIM)
    return o * lax.rsqrt(ms + EPS) * w


def _diff_kernel(lam_ref, q_ref, k_ref, v_ref, bias_ref, subw_ref, o_ref, *, tile, nq):
    lam = lam_ref[0, 0]
    lane = lax.broadcasted_iota(jnp.int32, (tile, LANES), 1)
    subw = subw_ref[...]
    rowmax = lambda t: jnp.max(t, axis=-1, keepdims=True)
    rowsum = lambda t: jnp.sum(t, axis=-1, keepdims=True)

    def scores(qi):
        q = q_ref[0, qi * tile:(qi + 1) * tile, :] * jnp.asarray(HEAD_DIM ** -0.5, BF16)
        zero = jnp.zeros_like(q)
        qq = jnp.concatenate([jnp.where(lane < HEAD_DIM, q, zero), jnp.where(lane >= HEAD_DIM, q, zero)], axis=0)
        return _nt_dot(qq, k_ref[0, :(qi + 1) * tile, :])

    def probs(qi, s):
        parts = [s[:, qi * tile:] + bias_ref[0, 0]]
        if qi >= 1:
            parts.append(s[:, (qi - 1) * tile:qi * tile] + bias_ref[0, 1])
        if qi >= 2:
            parts.append(s[:, :(qi - 1) * tile])
        m = functools.reduce(jnp.maximum, [rowmax(t) for t in parts])
        ps = [jnp.exp(t - m) for t in parts]
        l = functools.reduce(jnp.add, [rowsum(t) for t in ps])
        return jnp.concatenate([t.astype(BF16) for t in reversed(ps)], axis=1), l

    pending = {}
    for t in range(nq + 1):
        if t < nq:
            pending[t] = scores(t)
        if t >= 1:
            qi = t - 1
            s = pending.pop(qi)
            (p1, l1), (p2, l2) = probs(qi, s[:tile]), probs(qi, s[tile:])
            a = _dot(jnp.concatenate([p1, p2], axis=0), v_ref[0, :(qi + 1) * tile, :])
            o_ref[0, qi * tile:(qi + 1) * tile, :] = _diff_finish(
                a[:tile], l1, a[tile:], l2, lam, subw).astype(BF16)


def _diff_attn(lam, proj, bias_p, subw, n_heads):
    b, s, _ = proj.shape
    tile = ATT_TILE
    blk = lambda off: pl.BlockSpec((1, s, LANES), lambda i, h: (i, 0, off + h))
    return pl.pallas_call(
        functools.partial(_diff_kernel, tile=tile, nq=s // tile),
        grid=(b, n_heads),
        in_specs=[pl.BlockSpec(memory_space=pltpu.SMEM),
                  blk(0), blk(n_heads), blk(2 * n_heads),
                  pl.BlockSpec((1, 2, tile, tile), lambda i, h: (h, 0, 0, 0)),
                  pl.BlockSpec((1, LANES), lambda i, h: (0, 0))],
        out_specs=pl.BlockSpec((1, s, LANES), lambda i, h: (i, 0, h)),
        out_shape=jax.ShapeDtypeStruct((b, s, n_heads * LANES), BF16),
        compiler_params=pltpu.CompilerParams(vmem_limit_bytes=VMEM_LIMIT),
        name="diff_attn",
    )(lam, proj, proj, proj, bias_p, subw)


def _sb_logits(z2, tiles, run=None, total=False):
    neg_abs = lax.bitcast_convert_type(lax.bitcast_convert_type(z2, jnp.int32) | SIGN_BIT, F32)
    sp = jnp.log2(1.0 + jnp.exp2(neg_abs))
    lb = jnp.minimum(z2, 0.0) - sp
    l1m = lb - z2
    stops = np.cumsum([0] + [width for width, _, _ in tiles])
    xs = [None] * len(tiles)
    for j in range(len(tiles) - 1, -1, -1):
        width, tri2, mask = tiles[j]
        cols = slice(int(stops[j]), int(stops[j + 1]))
        lj = l1m[:, cols]
        if mask is not None:
            lj = jnp.where(mask, lj, 0.0)
        hi = lj.astype(BF16)
        lo = (lj - hi.astype(F32)).astype(BF16)
        if width % LANES == 0:
            suffix = _dot(jnp.concatenate([hi, lo], axis=1), tri2)
        else:
            suffix = _dot(hi, tri2) + _dot(lo, tri2)
        x = lb[:, cols] + suffix
        xs[j] = x if run is None else x + run
        if j > 0 or total:
            rs = jnp.sum(lj, axis=-1, keepdims=True)
            run = rs if run is None else run + rs
    return xs, run


def _sb_weights(xs, tiles):
    ws = []
    for x, (_, _, mask) in zip(xs, tiles):
        w = jnp.exp2(x)
        ws.append((w if mask is None else jnp.where(mask, w, 0.0)).astype(BF16))
    return jnp.concatenate(ws, axis=1) if len(ws) > 1 else ws[0]


def _strict_causal(n):
    return lax.broadcasted_iota(jnp.int32, (n, n), 1) < lax.broadcasted_iota(jnp.int32, (n, n), 0)


def _sb_kernel(q_ref, k_ref, v_ref, tri_ref, tris_ref, nw_ref, o_ref, *, tile, sub, nq):
    tri, tri_s = tri_ref[...], tris_ref[...]
    cmask, cmask_s = _strict_causal(tile), _strict_causal(sub)
    nw = nw_ref[...]
    scale = jnp.asarray(HEAD_DIM ** -0.5, BF16)

    def head_q(q, half):
        lane = lax.broadcasted_iota(jnp.int32, q.shape, 1)
        in_half = (lane >= half * HEAD_DIM) & (lane < (half + 1) * HEAD_DIM)
        return jnp.where(in_half, q, jnp.zeros_like(q))

    def finish(halves):
        lane = lax.broadcasted_iota(jnp.int32, halves[0].shape, 1)
        o = jnp.where(lane < HEAD_DIM, halves[0], halves[1])
        return _pair_rms(o, lane, nw).astype(BF16)

    def window(j):
        end = (j + 1) * sub
        start = max(end - sub - tile, 0)
        before = end - sub - start
        tiles = [(before, tri if before == tile else tri_s, None)] if before else []
        return start, end, tiles + [(sub, tri_s, cmask_s)]

    rows = [(j, half) for j in range(nq * tile // sub) for half in range(2)]
    z2s, xss, halves = {}, {}, {}
    worst = None
    for t in range(len(rows) + 2):
        if t < len(rows):
            j, half = rows[t]
            start, end, _ = window(j)
            q = q_ref[0, j * sub:(j + 1) * sub, :] * scale
            z2s[rows[t]] = _nt_dot(head_q(q, half), k_ref[0, start:end, :]) * LOG2E
        if 1 <= t <= len(rows):
            j, half = rows[t - 1]
            start, _, tiles = window(j)
            xss[j, half], run = _sb_logits(z2s.pop((j, half)), tiles, None, total=start > 0)
            if start > 0:
                worst = run if worst is None else jnp.maximum(worst, run)
        if t >= 2:
            j, half = rows[t - 2]
            start, end, tiles = window(j)
            halves[half] = _dot(_sb_weights(xss.pop((j, half)), tiles), v_ref[0, start:end, :])
            if half == 1:
                o_ref[0, j * sub:(j + 1) * sub, :] = finish(halves)

    if worst is None:
        return

    @pl.when(jnp.max(worst) >= SB_ZERO_LOG2)
    def _():
        def q_body(qi, carry):
            q0 = pl.multiple_of(qi * tile, tile)
            q = q_ref[0, pl.ds(q0, tile), :] * scale
            diag, plain = [(tile, tri, cmask)], [(tile, tri, None)]
            full = []
            for half in range(2):
                qm = head_q(q, half)
                xs, run = _sb_logits(_nt_dot(qm, k_ref[0, pl.ds(q0, tile), :]) * LOG2E, diag, None, total=True)
                acc = _dot(_sb_weights(xs, diag), v_ref[0, pl.ds(q0, tile), :])

                def kb_body(it, c):
                    acc, run = c
                    k0 = pl.multiple_of((qi - 1 - it) * tile, tile)
                    xs, run = _sb_logits(_nt_dot(qm, k_ref[0, pl.ds(k0, tile), :]) * LOG2E, plain, run, total=True)
                    return acc + _dot(_sb_weights(xs, plain), v_ref[0, pl.ds(k0, tile), :]), run

                acc, _ = lax.fori_loop(0, qi, kb_body, (acc, run))
                full.append(acc)
            o_ref[0, pl.ds(q0, tile), :] = finish(full)
            return carry

        lax.fori_loop(1, nq, q_body, 0)


def _sb_attn(proj, tri, tri_s, nw2, n_pairs, off):
    b, s, _ = proj.shape
    tile = ATT_TILE
    blk = lambda o: pl.BlockSpec((1, s, LANES), lambda i, p: (i, 0, o + p))
    return pl.pallas_call(
        functools.partial(_sb_kernel, tile=tile, sub=SB_SUB, nq=s // tile),
        grid=(b, n_pairs),
        in_specs=[blk(off), blk(off + n_pairs), blk(off + 2 * n_pairs),
                  pl.BlockSpec(tri.shape, lambda i, p: (0, 0)),
                  pl.BlockSpec(tri_s.shape, lambda i, p: (0, 0)),
                  pl.BlockSpec((1, LANES), lambda i, p: (0, 0))],
        out_specs=pl.BlockSpec((1, s, LANES), lambda i, p: (i, 0, p)),
        out_shape=jax.ShapeDtypeStruct((b, s, n_pairs * LANES), BF16),
        compiler_params=pltpu.CompilerParams(vmem_limit_bytes=VMEM_LIMIT),
        name="sb_attn",
    )(proj, proj, proj, tri, tri_s, nw2)


def _sample_attn_kernel(lam_ref, qd_ref, kd_ref, vd_ref, qs_ref, ks_ref, vs_ref,
                        cdk_ref, cdv_ref, csk_ref, csv_ref, bsp_ref, bsn_ref,
                        subw_ref, nw_ref, tri_ref, od_ref, os_ref, *, sq, past, tile, heads):
    lam = lam_ref[0, 0]
    scale = jnp.asarray(HEAD_DIM ** -0.5, BF16)
    lane = lax.broadcasted_iota(jnp.int32, (sq, LANES), 1)
    cmask = _strict_causal(sq)
    tri2 = tri_ref[...]
    tri_new = tri2[:sq, :sq]
    halves_of = [slice(hh * HEAD_DIM, (hh + 1) * HEAD_DIM) for hh in range(2)]

    for h in range(heads):
        cols = slice(h * LANES, (h + 1) * LANES)

        q = qd_ref[0][:, cols] * scale
        kc = cdk_ref[0, 0, h].astype(BF16)
        vc = cdv_ref[0, 0, h].astype(BF16)
        kn = kd_ref[0][:, cols]
        vn = vd_ref[0][:, cols]
        outs = []
        for qm in (jnp.where(lane < HEAD_DIM, q, jnp.zeros_like(q)),
                   jnp.where(lane >= HEAD_DIM, q, jnp.zeros_like(q))):
            sc = _nt_dot(qm, kc) + bsp_ref[h]
            sn = _nt_dot(qm, kn) + bsn_ref[h]
            m = jnp.maximum(jnp.max(sc, axis=-1, keepdims=True), jnp.max(sn, axis=-1, keepdims=True))
            pc = jnp.exp(sc - m)
            pn = jnp.exp(sn - m)
            l = jnp.sum(pc, axis=-1, keepdims=True) + jnp.sum(pn, axis=-1, keepdims=True)
            outs.append((_dot(pc.astype(BF16), vc) + _dot(pn.astype(BF16), vn), l))
        (a1, l1), (a2, l2) = outs
        od_ref[0, :, cols] = _diff_finish(a1, l1, a2, l2, lam, subw_ref[...]).astype(BF16)

        qs = qs_ref[0][:, cols] * scale
        ks = ks_ref[0][:, cols]
        vs = vs_ref[0][:, cols]
        z_new = [_nt_dot(qs[:, sl], ks[:, sl]) * LOG2E for sl in halves_of]
        z_old = [_dot(qs[:, sl], csk_ref[0, 0, 2 * h + hh].astype(BF16)) * LOG2E
                 for hh, sl in enumerate(halves_of)]
        halves = []
        for hh, sl in enumerate(halves_of):
            new, old = [(sq, tri_new, cmask)], [(tile, tri2, None)] * (past // tile)
            x_new, run = _sb_logits(z_new[hh], new, None, total=True)
            x_old, _ = _sb_logits(z_old[hh], old, run)
            halves.append(_dot(_sb_weights(x_new, new), vs[:, sl])
                          + _nt_dot(_sb_weights(x_old, old), csv_ref[0, 0, 2 * h + hh].astype(BF16)))
        o = jnp.concatenate(halves, axis=-1)
        os_ref[0, :, cols] = _pair_rms(o, lane, nw_ref[...]).astype(BF16)


def _sample_attn(lam, proj, cdk, cdv, csk, csv, bias_sp, bias_sn, subw, nw2, tri, n_dh):
    b, sq, _ = proj.shape
    past = cdk.shape[3]
    tile = ATT_TILE
    wide = n_dh * LANES
    blk = lambda off: pl.BlockSpec((1, sq, wide), lambda i: (i, 0, off))
    cd = pl.BlockSpec((1, 1, n_dh, past, LANES), lambda i: (0, i, 0, 0, 0))
    cs = pl.BlockSpec((1, 1, 2 * n_dh, HEAD_DIM, past), lambda i: (0, i, 0, 0, 0))
    full = lambda a: pl.BlockSpec(a.shape, lambda i: (0,) * a.ndim)
    out = pl.BlockSpec((1, sq, wide), lambda i: (i, 0, 0))
    return pl.pallas_call(
        functools.partial(_sample_attn_kernel, sq=sq, past=past, tile=tile, heads=n_dh),
        grid=(b,),
        in_specs=[pl.BlockSpec(memory_space=pltpu.SMEM),
                  blk(0), blk(1), blk(2), blk(3), blk(4), blk(5),
                  cd, cd, cs, cs, full(bias_sp), full(bias_sn), full(subw), full(nw2), full(tri)],
        out_specs=[out, out],
        out_shape=[jax.ShapeDtypeStruct((b, sq, wide), BF16)] * 2,
        compiler_params=pltpu.CompilerParams(vmem_limit_bytes=VMEM_LIMIT),
        name="sample_attn",
    )(lam, proj, proj, proj, proj, proj, proj, cdk, cdv, csk, csv, bias_sp, bias_sn, subw, nw2, tri)


CARRY_ROWS = 8


def _mlp_kernel(x_ref, od_ref, os_ref, cinit_ref, wod_ref, wos_ref, fnw_ref, wup_ref, cw_ref, cb_ref,
                wdn_ref, finw_ref, y_ref, nconv_ref, carry_ref, *, ts, ff, fc):
    @pl.when(pl.program_id(1) == 0)
    def _():
        carry_ref[...] = cinit_ref[0]

    x1 = x_ref[0] + _dot(od_ref[0], wod_ref[...]) + _dot(os_ref[0], wos_ref[...])
    h2 = _rms(x1, fnw_ref[...]).astype(BF16)
    groups = ts // CARRY_ROWS
    sub = lax.broadcasted_iota(jnp.int32, (groups, CARRY_ROWS, fc), 1)
    nc = ff // fc

    def up(c):
        return tuple(_dot(h2, wup_ref[:, c0:c0 + fc]) for c0 in (c * fc, ff + c * fc))

    def conv(u, c0):
        cols = slice(c0, c0 + fc)
        u3 = u.reshape(groups, CARRY_ROWS, fc)
        before = carry_ref[:, cols]

        def shifted(k):
            rot = pltpu.roll(u3, k, 1)
            above = pltpu.roll(before, k, 0)[None]
            if groups > 1:
                above = jnp.concatenate([above, rot[:-1]], axis=0)
            return jnp.where(sub < k, above, rot).reshape(ts, fc)

        out = (cb_ref[:, cols] + shifted(2) * cw_ref[0:1, cols] + shifted(1) * cw_ref[1:2, cols]
               + u * cw_ref[2:3, cols])
        tail = u[ts - CARRY_ROWS:, :]
        carry_ref[:, cols] = tail
        nconv_ref[0, :, cols] = tail
        return out

    acc = None
    u = up(0)
    acts = []

    def down(c_end):
        act = jnp.concatenate(acts, axis=1) if len(acts) > 1 else acts[0]
        part = _dot(act, wdn_ref[(c_end - len(acts)) * fc:c_end * fc, :])
        return part if acc is None else acc + part

    for c in range(nc):
        u_next = up(c + 1) if c + 1 < nc else None
        if len(acts) == 2:
            acc = down(c)
            acts = []
        g = conv(u[0], c * fc)
        v = conv(u[1], ff + c * fc)
        acts.append((g / (1.0 + jnp.exp(-g)) * v).astype(BF16))
        u = u_next
    y_ref[0] = _rms(x1 + down(nc), finw_ref[...])


def _mlp(x, od, osb, cinit, wod, wos, fnw, wup, cw, cb, wdn, finw, ts):
    b, s, d = x.shape
    ff = wdn.shape[0]
    half = od.shape[2]
    fc = 256
    const = lambda a: pl.BlockSpec(a.shape, lambda i, j: (0, 0))
    tok = lambda w: pl.BlockSpec((1, ts, w), lambda i, j: (i, j, 0))
    state = pl.BlockSpec((1, CARRY_ROWS, 2 * ff), lambda i, j: (i, 0, 0))
    return pl.pallas_call(
        functools.partial(_mlp_kernel, ts=ts, ff=ff, fc=fc),
        grid=(b, s // ts),
        in_specs=[tok(d), tok(half), tok(half), state, const(wod), const(wos), const(fnw), const(wup),
                  const(cw), const(cb), const(wdn), const(finw)],
        out_specs=[tok(d), state],
        out_shape=[jax.ShapeDtypeStruct((b, s, d), F32),
                   jax.ShapeDtypeStruct((b, CARRY_ROWS, 2 * ff), F32)],
        scratch_shapes=[pltpu.VMEM((CARRY_ROWS, 2 * ff), F32)],
        compiler_params=pltpu.CompilerParams(dimension_semantics=("arbitrary", "arbitrary"),
                                             vmem_limit_bytes=VMEM_LIMIT),
        name="mlp",
    )(x, od, osb, cinit, wod, wos, fnw, wup, cw, cb, wdn, finw)


def kernel(x_prompt, x_sample, cache_diff_k, cache_diff_v, cache_sb_k, cache_sb_v, state_conv, attn_norm_w, w_in, lambda_q1, lambda_k1, lambda_q2, lambda_k2, diff_subln_w, sb_norm_w, w_out, ffn_norm_w, w_up, conv_w, conv_b, w_down, rel_bias, final_norm_w):
    depth = w_in.shape[0]
    assert depth == 1, "single-layer step"
    bp, sp, d = x_prompt.shape
    bs, ss, _ = x_sample.shape
    past = cache_diff_k.shape[3]
    n_dh = cache_diff_k.shape[2]
    n_sh = cache_sb_k.shape[2]
    ff = w_down.shape[1]
    tile = ATT_TILE
    assert sp % tile == 0 and past % tile == 0 and tile % CHUNK == 0 and n_sh == 2 * n_dh
    assert ss % CARRY_ROWS == 0
    lam_init = 0.8 - 0.6 * math.exp(-0.3 * 0)

    loc = np.arange(tile)
    bkt_p = np.stack([_bucket_tiles(loc + tile, loc + tile), _bucket_tiles(loc + tile, loc)])
    far_bucket = int(_rel_bucket_np(np.array([-(tile + 1)]))[0])
    assert far_bucket == int(_rel_bucket_np(np.array([-(sp + past)]))[0])
    qpos_s = past + np.arange(ss)
    bkt_sp = _bucket_tiles(qpos_s, np.arange(past))
    bkt_sn = _bucket_tiles(qpos_s, qpos_s)

    row = lambda v: v.reshape(1, -1).astype(F32)
    bias_p, bias_sp, bias_sn, lam = _prep(
        rel_bias.astype(F32), row(lambda_q1[0]), row(lambda_k1[0]), row(lambda_q2[0]), row(lambda_k2[0]),
        jnp.asarray(bkt_p), jnp.asarray(bkt_sp), jnp.asarray(bkt_sn), lam_init, far_bucket)

    w_in_bf = w_in[0].astype(BF16)
    w_out_bf = w_out[0].astype(BF16)
    half = n_dh * LANES
    wod, wos = w_out_bf[:half], w_out_bf[half:]
    wup = w_up[0].astype(BF16)
    wdn = w_down[0].astype(BF16)
    anw, fnw, finw = row(attn_norm_w[0]), row(ffn_norm_w[0]), row(final_norm_w)
    subw = row(diff_subln_w[0]) * (1.0 - lam_init)
    nw2 = jnp.tile(row(sb_norm_w[0]), (1, 2))
    cw, cb = conv_w[0].astype(F32), row(conv_b[0])
    def suffix_matrix(n):
        m = np.triu(np.ones((n, n), np.float32), 1).T
        return jnp.asarray(np.concatenate([m, m], axis=0), BF16)

    tri, tri_s = suffix_matrix(tile), suffix_matrix(SB_SUB)

    def state8(st):
        return jnp.pad(st, ((0, 0), (CARRY_ROWS - st.shape[1], 0), (0, 0)))

    ts_p = min(512, sp)
    proj_p, p_dk, p_dv, p_sk, p_sv = _inproj(x_prompt, anw, w_in_bf, ts_p)
    od_p = _diff_attn(lam, proj_p, bias_p, subw, n_dh)
    os_p = _sb_attn(proj_p, tri, tri_s, nw2, n_dh, 3 * n_dh)
    y_p, cv_p = _mlp(x_prompt, od_p, os_p, jnp.zeros((bp, CARRY_ROWS, 2 * ff), F32),
                     wod, wos, fnw, wup, cw, cb, wdn, finw, min(1024, sp))

    proj_s, s_dk, s_dv, s_sk, s_sv = _inproj(x_sample, anw, w_in_bf, bs * ss, seqs=bs)
    od_s, os_s = _sample_attn(lam, proj_s, cache_diff_k, cache_diff_v,
                              jnp.swapaxes(cache_sb_k, 3, 4), jnp.swapaxes(cache_sb_v, 3, 4),
                              bias_sp, bias_sn, subw, nw2, tri, n_dh)
    y_s, cv_s = _mlp(x_sample, od_s, os_s, state8(state_conv[0]),
                     wod, wos, fnw, wup, cw, cb, wdn, finw, ss)

    tail = lambda cv: cv[None, :, CARRY_ROWS - 2:, :]
    return (y_p, y_s, p_dk, p_dv, p_sk, p_sv, tail(cv_p), s_dk, s_dv, s_sk, s_sv, tail(cv_s))
```

```python
import functools
import math

import numpy as np
import jax
import jax.numpy as jnp
from jax import lax
from jax.experimental import pallas as pl
from jax.experimental.pallas import tpu as pltpu

F32 = jnp.float32
BF16 = jnp.bfloat16

HEAD_DIM = 64
LANES = 128
CHUNK = 64
N_BUCKETS = 32
EPS = 1e-6
NEG = -1e30
LOG2E = 1.4426950408889634
SIGN_BIT = np.int32(-2 ** 31)
SB_ZERO_LOG2 = -160.0
ATT_TILE = 256
SB_SUB = ATT_TILE // 2
VMEM_LIMIT = 60 * 1024 * 1024


def _rel_bucket_np(rel):
    half = N_BUCKETS // 2
    max_exact = half // 2
    n = np.abs(rel)
    thresholds = (12, 16, 23, 32, 46, 64, 91)
    large = max_exact + sum((n >= t).astype(np.int64) for t in thresholds)
    return (np.where(rel > 0, half, 0) + np.where(n < max_exact, n, large)).astype(np.int32)


def _bucket_tiles(qpos, kpos):
    rel = kpos[None, :] - qpos[:, None]
    vis = (kpos[None, :] // CHUNK) <= (qpos[:, None] // CHUNK)
    return np.where(vis, _rel_bucket_np(rel), -1).astype(np.int32)


def _nt_dot(a, b):
    return lax.dot_general(a, b, (((1,), (1,)), ((), ())), preferred_element_type=F32)


def _dot(a, b):
    return jnp.dot(a, b, preferred_element_type=F32)


def _rms(x, w):
    return x * lax.rsqrt(jnp.mean(x * x, axis=-1, keepdims=True) + EPS) * w


def _prep_kernel(rb_ref, lq1_ref, lk1_ref, lq2_ref, lk2_ref, bp_ref, bsp_ref, bsn_ref,
                 bias_p_ref, bias_sp_ref, bias_sn_ref, lam_ref, *, lam_init, far_bucket):
    h = pl.program_id(0)
    pairs = ((bp_ref, bias_p_ref), (bsp_ref, bias_sp_ref), (bsn_ref, bias_sn_ref))
    for _, out in pairs:
        out[...] = jnp.full(out.shape, NEG, F32)

    def body(b, carry):
        val = rb_ref[b, h]
        for idx, out in pairs:
            out[0] = jnp.where(idx[...] == b, val, out[0])
        return carry

    lax.fori_loop(0, N_BUCKETS, body, 0)
    bias_p_ref[0] = bias_p_ref[0] - rb_ref[far_bucket, h]
    s1 = jnp.sum(lq1_ref[...] * lk1_ref[...], axis=-1, keepdims=True)
    s2 = jnp.sum(lq2_ref[...] * lk2_ref[...], axis=-1, keepdims=True)
    lam_ref[...] = jnp.exp(s1) - jnp.exp(s2) + lam_init


def _prep(rel_bias, lq1, lk1, lq2, lk2, bkt_p, bkt_sp, bkt_sn, lam_init, far_bucket):
    nh = rel_bias.shape[1]
    full = lambda a: pl.BlockSpec(a.shape, lambda h: (0,) * a.ndim)
    head = lambda a: pl.BlockSpec((1,) + a.shape, lambda h: (h,) + (0,) * a.ndim)
    vec = pl.BlockSpec((1, HEAD_DIM), lambda h: (0, 0))
    return pl.pallas_call(
        functools.partial(_prep_kernel, lam_init=lam_init, far_bucket=far_bucket),
        grid=(nh,),
        in_specs=[pl.BlockSpec(memory_space=pltpu.SMEM), vec, vec, vec, vec,
                  full(bkt_p), full(bkt_sp), full(bkt_sn)],
        out_specs=[head(bkt_p), head(bkt_sp), head(bkt_sn), pl.BlockSpec((1, 1), lambda h: (0, 0))],
        out_shape=[jax.ShapeDtypeStruct((nh,) + bkt_p.shape, F32),
                   jax.ShapeDtypeStruct((nh,) + bkt_sp.shape, F32),
                   jax.ShapeDtypeStruct((nh,) + bkt_sn.shape, F32),
                   jax.ShapeDtypeStruct((1, 1), F32)],
        name="prep",
    )(rel_bias, lq1, lk1, lq2, lk2, bkt_p, bkt_sp, bkt_sn)


def _inproj_kernel(x_ref, nw_ref, w_ref, proj_ref, dk_ref, dv_ref, sk_ref, sv_ref, *, width, seqs):
    h = _rms(x_ref[0], nw_ref[...]).astype(BF16)
    rows = h.shape[0] // seqs
    for g in (4, 5, 1, 2, 3, 0):
        acc = _dot(h, w_ref[:, g * width:(g + 1) * width])
        proj_ref[0, :, g * width:(g + 1) * width] = acc.astype(BF16)
        out, hd = {1: (dk_ref, LANES), 2: (dv_ref, LANES), 4: (sk_ref, HEAD_DIM), 5: (sv_ref, HEAD_DIM)}.get(
            g, (None, None))
        if out is not None:
            for q in range(seqs):
                for hh in range(width // hd):
                    out[0, q, hh] = acc[q * rows:(q + 1) * rows, hh * hd:(hh + 1) * hd]


def _inproj(x, nw, w_in_bf, ts, seqs=1):
    b, s, d = x.shape
    assert seqs == 1 or ts == seqs * s
    cols = w_in_bf.shape[1]
    width = cols // 6
    n_dh, n_sh = width // LANES, width // HEAD_DIM
    xg = x.reshape(b // seqs, seqs * s, d)
    kv = lambda nh, hd: pl.BlockSpec((1, seqs, nh, ts // seqs, hd), lambda i, j: (0, i, 0, j, 0))
    outs = pl.pallas_call(
        functools.partial(_inproj_kernel, width=width, seqs=seqs),
        grid=(b // seqs, seqs * s // ts),
        in_specs=[pl.BlockSpec((1, ts, d), lambda i, j: (i, j, 0)),
                  pl.BlockSpec((1, d), lambda i, j: (0, 0)),
                  pl.BlockSpec((d, cols), lambda i, j: (0, 0))],
        out_specs=[pl.BlockSpec((1, ts, cols), lambda i, j: (i, j, 0)),
                   kv(n_dh, LANES), kv(n_dh, LANES), kv(n_sh, HEAD_DIM), kv(n_sh, HEAD_DIM)],
        out_shape=[jax.ShapeDtypeStruct(xg.shape[:2] + (cols,), BF16),
                   jax.ShapeDtypeStruct((1, b, n_dh, s, LANES), F32),
                   jax.ShapeDtypeStruct((1, b, n_dh, s, LANES), F32),
                   jax.ShapeDtypeStruct((1, b, n_sh, s, HEAD_DIM), F32),
                   jax.ShapeDtypeStruct((1, b, n_sh, s, HEAD_DIM), F32)],
        compiler_params=pltpu.CompilerParams(vmem_limit_bytes=VMEM_LIMIT),
        name="inproj",
    )(xg, nw, w_in_bf)
    return (outs[0].reshape(b, s, cols),) + tuple(outs[1:])


def _diff_finish(a1, l1, a2, l2, lam, subw):
    o = a1 / l1 - lam * (a2 / l2)
    return _rms(o, subw)


def _pair_rms(o, lane, w):
    o2 = o * o
    tot = jnp.sum(o2, axis=-1, keepdims=True)
    lo = jnp.sum(jnp.where(lane < HEAD_DIM, o2, 0.0), axis=-1, keepdims=True)
    ms = jnp.where(lane < HEAD_DIM, lo, tot - lo) * (1.0 / HEAD_DIM)
    return o * lax.rsqrt(ms + EPS) * w


def _diff_kernel(lam_ref, q_ref, k_ref, v_ref, bias_ref, subw_ref, o_ref, *, tile, nq):
    lam = lam_ref[0, 0]
    lane = lax.broadcasted_iota(jnp.int32, (tile, LANES), 1)
    subw = subw_ref[...]
    rowmax = lambda t: jnp.max(t, axis=-1, keepdims=True)
    rowsum = lambda t: jnp.sum(t, axis=-1, keepdims=True)

    def scores(qi):
        q = q_ref[0, qi * tile:(qi + 1) * tile, :] * jnp.asarray(HEAD_DIM ** -0.5, BF16)
        zero = jnp.zeros_like(q)
        qq = jnp.concatenate([jnp.where(lane < HEAD_DIM, q, zero), jnp.where(lane >= HEAD_DIM, q, zero)], axis=0)
        return _nt_dot(qq, k_ref[0, :(qi + 1) * tile, :])

    def probs(qi, s):
        parts = [s[:, qi * tile:] + bias_ref[0, 0]]
        if qi >= 1:
            parts.append(s[:, (qi - 1) * tile:qi * tile] + bias_ref[0, 1])
        if qi >= 2:
            parts.append(s[:, :(qi - 1) * tile])
        m = functools.reduce(jnp.maximum, [rowmax(t) for t in parts])
        ps = [jnp.exp(t - m) for t in parts]
        l = functools.reduce(jnp.add, [rowsum(t) for t in ps])
        return jnp.concatenate([t.astype(BF16) for t in reversed(ps)], axis=1), l

    pending = {}
    for t in range(nq + 1):
        if t < nq:
            pending[t] = scores(t)
        if t >= 1:
            qi = t - 1
            s = pending.pop(qi)
            (p1, l1), (p2, l2) = probs(qi, s[:tile]), probs(qi, s[tile:])
            a = _dot(jnp.concatenate([p1, p2], axis=0), v_ref[0, :(qi + 1) * tile, :])
            o_ref[0, qi * tile:(qi + 1) * tile, :] = _diff_finish(
                a[:tile], l1, a[tile:], l2, lam, subw).astype(BF16)


def _diff_attn(lam, proj, bias_p, subw, n_heads):
    b, s, _ = proj.shape
    tile = ATT_TILE
    blk = lambda off: pl.BlockSpec((1, s, LANES), lambda i, h: (i, 0, off + h))
    return pl.pallas_call(
        functools.partial(_diff_kernel, tile=tile, nq=s // tile),
        grid=(b, n_heads),
        in_specs=[pl.BlockSpec(memory_space=pltpu.SMEM),
                  blk(0), blk(n_heads), blk(2 * n_heads),
                  pl.BlockSpec((1, 2, tile, tile), lambda i, h: (h, 0, 0, 0)),
                  pl.BlockSpec((1, LANES), lambda i, h: (0, 0))],
        out_specs=pl.BlockSpec((1, s, LANES), lambda i, h: (i, 0, h)),
        out_shape=jax.ShapeDtypeStruct((b, s, n_heads * LANES), BF16),
        compiler_params=pltpu.CompilerParams(vmem_limit_bytes=VMEM_LIMIT),
        name="diff_attn",
    )(lam, proj, proj, proj, bias_p, subw)


def _sb_logits(z2, tiles, run=None, total=False):
    neg_abs = lax.bitcast_convert_type(lax.bitcast_convert_type(z2, jnp.int32) | SIGN_BIT, F32)
    sp = jnp.log2(1.0 + jnp.exp2(neg_abs))
    lb = jnp.minimum(z2, 0.0) - sp
    l1m = lb - z2
    stops = np.cumsum([0] + [width for width, _, _ in tiles])
    xs = [None] * len(tiles)
    for j in range(len(tiles) - 1, -1, -1):
        width, tri2, mask = tiles[j]
        cols = slice(int(stops[j]), int(stops[j + 1]))
        lj = l1m[:, cols]
        if mask is not None:
            lj = jnp.where(mask, lj, 0.0)
        hi = lj.astype(BF16)
        lo = (lj - hi.astype(F32)).astype(BF16)
        if width % LANES == 0:
            suffix = _dot(jnp.concatenate([hi, lo], axis=1), tri2)
        else:
            suffix = _dot(hi, tri2) + _dot(lo, tri2)
        x = lb[:, cols] + suffix
        xs[j] = x if run is None else x + run
        if j > 0 or total:
            rs = jnp.sum(lj, axis=-1, keepdims=True)
            run = rs if run is None else run + rs
    return xs, run


def _sb_weights(xs, tiles):
    ws = []
    for x, (_, _, mask) in zip(xs, tiles):
        w = jnp.exp2(x)
        ws.append((w if mask is None else jnp.where(mask, w, 0.0)).astype(BF16))
    return jnp.concatenate(ws, axis=1) if len(ws) > 1 else ws[0]


def _strict_causal(n):
    return lax.broadcasted_iota(jnp.int32, (n, n), 1) < lax.broadcasted_iota(jnp.int32, (n, n), 0)


def _sb_kernel(q_ref, k_ref, v_ref, tri_ref, tris_ref, nw_ref, o_ref, *, tile, sub, nq):
    tri, tri_s = tri_ref[...], tris_ref[...]
    cmask = _strict_causal(tile)
    row2 = lax.broadcasted_iota(jnp.int32, (2 * sub, sub), 0) & (sub - 1)
    cmask_s = lax.broadcasted_iota(jnp.int32, (2 * sub, sub), 1) < row2
    nw = nw_ref[...]
    scale = jnp.asarray(HEAD_DIM ** -0.5, BF16)

    def head_q(q, half):
        lane = lax.broadcasted_iota(jnp.int32, q.shape, 1)
        in_half = (lane >= half * HEAD_DIM) & (lane < (half + 1) * HEAD_DIM)
        return jnp.where(in_half, q, jnp.zeros_like(q))

    def finish(halves):
        lane = lax.broadcasted_iota(jnp.int32, halves[0].shape, 1)
        o = jnp.where(lane < HEAD_DIM, halves[0], halves[1])
        return _pair_rms(o, lane, nw).astype(BF16)

    def window(j):
        end = (j + 1) * sub
        start = max(end - sub - tile, 0)
        before = end - sub - start
        tiles = [(before, tri if before == tile else tri_s, None)] if before else []
        return start, end, tiles + [(sub, tri_s, cmask_s)]

    n_sub = nq * tile // sub
    z2s, xss = {}, {}
    worst = None
    for t in range(n_sub + 2):
        if t < n_sub:
            start, end, _ = window(t)
            q = q_ref[0, t * sub:(t + 1) * sub, :] * scale
            qq = jnp.concatenate([head_q(q, 0), head_q(q, 1)], axis=0)
            z2s[t] = _nt_dot(qq, k_ref[0, start:end, :]) * LOG2E
        if 1 <= t <= n_sub:
            start, _, tiles = window(t - 1)
            xss[t - 1], run = _sb_logits(z2s.pop(t - 1), tiles, None, total=start > 0)
            if start > 0:
                worst = run if worst is None else jnp.maximum(worst, run)
        if t >= 2:
            j = t - 2
            start, end, tiles = window(j)
            a = _dot(_sb_weights(xss.pop(j), tiles), v_ref[0, start:end, :])
            o_ref[0, j * sub:(j + 1) * sub, :] = finish([a[:sub], a[sub:]])

    if worst is None:
        return

    @pl.when(jnp.max(worst) >= SB_ZERO_LOG2)
    def _():
        def q_body(qi, carry):
            q0 = pl.multiple_of(qi * tile, tile)
            q = q_ref[0, pl.ds(q0, tile), :] * scale
            diag, plain = [(tile, tri, cmask)], [(tile, tri, None)]
            full = []
            for half in range(2):
                qm = head_q(q, half)
                xs, run = _sb_logits(_nt_dot(qm, k_ref[0, pl.ds(q0, tile), :]) * LOG2E, diag, None, total=True)
                acc = _dot(_sb_weights(xs, diag), v_ref[0, pl.ds(q0, tile), :])

                def kb_body(it, c):
                    acc, run = c
                    k0 = pl.multiple_of((qi - 1 - it) * tile, tile)
                    xs, run = _sb_logits(_nt_dot(qm, k_ref[0, pl.ds(k0, tile), :]) * LOG2E, plain, run, total=True)
                    return acc + _dot(_sb_weights(xs, plain), v_ref[0, pl.ds(k0, tile), :]), run

                acc, _ = lax.fori_loop(0, qi, kb_body, (acc, run))
                full.append(acc)
            o_ref[0, pl.ds(q0, tile), :] = finish(full)
            return carry

        lax.fori_loop(1, nq, q_body, 0)


def _sb_attn(proj, tri, tri_s, nw2, n_pairs, off):
    b, s, _ = proj.shape
    tile = ATT_TILE
    blk = lambda o: pl.BlockSpec((1, s, LANES), lambda i, p: (i, 0, o + p))
    return pl.pallas_call(
        functools.partial(_sb_kernel, tile=tile, sub=SB_SUB, nq=s // tile),
        grid=(b, n_pairs),
        in_specs=[blk(off), blk(off + n_pairs), blk(off + 2 * n_pairs),
                  pl.BlockSpec(tri.shape, lambda i, p: (0, 0)),
                  pl.BlockSpec(tri_s.shape, lambda i, p: (0, 0)),
                  pl.BlockSpec((1, LANES), lambda i, p: (0, 0))],
        out_specs=pl.BlockSpec((1, s, LANES), lambda i, p: (i, 0, p)),
        out_shape=jax.ShapeDtypeStruct((b, s, n_pairs * LANES), BF16),
        compiler_params=pltpu.CompilerParams(vmem_limit_bytes=VMEM_LIMIT),
        name="sb_attn",
    )(proj, proj, proj, tri, tri_s, nw2)


def _sample_attn_kernel(lam_ref, qd_ref, kd_ref, vd_ref, qs_ref, ks_ref, vs_ref,
                        cdk_ref, cdv_ref, csk_ref, csv_ref, bsp_ref, bsn_ref,
                        subw_ref, nw_ref, tri_ref, od_ref, os_ref, *, sq, past, tile, heads):
    lam = lam_ref[0, 0]
    scale = jnp.asarray(HEAD_DIM ** -0.5, BF16)
    lane = lax.broadcasted_iota(jnp.int32, (sq, LANES), 1)
    cmask = _strict_causal(sq)
    tri2 = tri_ref[...]
    tri_new = tri2[:sq, :sq]
    halves_of = [slice(hh * HEAD_DIM, (hh + 1) * HEAD_DIM) for hh in range(2)]

    for h in range(heads):
        cols = slice(h * LANES, (h + 1) * LANES)

        q = qd_ref[0][:, cols] * scale
        kc = cdk_ref[0, 0, h].astype(BF16)
        vc = cdv_ref[0, 0, h].astype(BF16)
        kn = kd_ref[0][:, cols]
        vn = vd_ref[0][:, cols]
        outs = []
        for qm in (jnp.where(lane < HEAD_DIM, q, jnp.zeros_like(q)),
                   jnp.where(lane >= HEAD_DIM, q, jnp.zeros_like(q))):
            sc = _nt_dot(qm, kc) + bsp_ref[h]
            sn = _nt_dot(qm, kn) + bsn_ref[h]
            m = jnp.maximum(jnp.max(sc, axis=-1, keepdims=True), jnp.max(sn, axis=-1, keepdims=True))
            pc = jnp.exp(sc - m)
            pn = jnp.exp(sn - m)
            l = jnp.sum(pc, axis=-1, keepdims=True) + jnp.sum(pn, axis=-1, keepdims=True)
            outs.append((_dot(pc.astype(BF16), vc) + _dot(pn.astype(BF16), vn), l))
        (a1, l1), (a2, l2) = outs
        od_ref[0, :, cols] = _diff_finish(a1, l1, a2, l2, lam, subw_ref[...]).astype(BF16)

        qs = qs_ref[0][:, cols] * scale
        ks = ks_ref[0][:, cols]
        vs = vs_ref[0][:, cols]
        z_new = [_nt_dot(qs[:, sl], ks[:, sl]) * LOG2E for sl in halves_of]
        z_old = [_dot(qs[:, sl], csk_ref[0, 0, 2 * h + hh].astype(BF16)) * LOG2E
                 for hh, sl in enumerate(halves_of)]
        halves = []
        for hh, sl in enumerate(halves_of):
            new, old = [(sq, tri_new, cmask)], [(tile, tri2, None)] * (past // tile)
            x_new, run = _sb_logits(z_new[hh], new, None, total=True)
            x_old, _ = _sb_logits(z_old[hh], old, run)
            halves.append(_dot(_sb_weights(x_new, new), vs[:, sl])
                          + _nt_dot(_sb_weights(x_old, old), csv_ref[0, 0, 2 * h + hh].astype(BF16)))
        o = jnp.concatenate(halves, axis=-1)
        os_ref[0, :, cols] = _pair_rms(o, lane, nw_ref[...]).astype(BF16)


def _sample_attn(lam, proj, cdk, cdv, csk, csv, bias_sp, bias_sn, subw, nw2, tri, n_dh):
    b, sq, _ = proj.shape
    past = cdk.shape[3]
    tile = ATT_TILE
    wide = n_dh * LANES
    blk = lambda off: pl.BlockSpec((1, sq, wide), lambda i: (i, 0, off))
    cd = pl.BlockSpec((1, 1, n_dh, past, LANES), lambda i: (0, i, 0, 0, 0))
    cs = pl.BlockSpec((1, 1, 2 * n_dh, HEAD_DIM, past), lambda i: (0, i, 0, 0, 0))
    full = lambda a: pl.BlockSpec(a.shape, lambda i: (0,) * a.ndim)
    out = pl.BlockSpec((1, sq, wide), lambda i: (i, 0, 0))
    return pl.pallas_call(
        functools.partial(_sample_attn_kernel, sq=sq, past=past, tile=tile, heads=n_dh),
        grid=(b,),
        in_specs=[pl.BlockSpec(memory_space=pltpu.SMEM),
                  blk(0), blk(1), blk(2), blk(3), blk(4), blk(5),
                  cd, cd, cs, cs, full(bias_sp), full(bias_sn), full(subw), full(nw2), full(tri)],
        out_specs=[out, out],
        out_shape=[jax.ShapeDtypeStruct((b, sq, wide), BF16)] * 2,
        compiler_params=pltpu.CompilerParams(vmem_limit_bytes=VMEM_LIMIT),
        name="sample_attn",
    )(lam, proj, proj, proj, proj, proj, proj, cdk, cdv, csk, csv, bias_sp, bias_sn, subw, nw2, tri)


CARRY_ROWS = 8


def _mlp_kernel(x_ref, od_ref, os_ref, cinit_ref, wod_ref, wos_ref, fnw_ref, wup_ref, cw_ref, cb_ref,
                wdn_ref, finw_ref, y_ref, nconv_ref, carry_ref, *, ts, ff, fc):
    @pl.when(pl.program_id(1) == 0)
    def _():
        carry_ref[...] = cinit_ref[0]

    x1 = x_ref[0] + _dot(od_ref[0], wod_ref[...]) + _dot(os_ref[0], wos_ref[...])
    h2 = _rms(x1, fnw_ref[...]).astype(BF16)
    groups = ts // CARRY_ROWS
    sub = lax.broadcasted_iota(jnp.int32, (groups, CARRY_ROWS, fc), 1)
    nc = ff // fc

    def up(c):
        return tuple(_dot(h2, wup_ref[:, c0:c0 + fc]) for c0 in (c * fc, ff + c * fc))

    def conv(u, c0):
        cols = slice(c0, c0 + fc)
        u3 = u.reshape(groups, CARRY_ROWS, fc)
        before = carry_ref[:, cols]

        def shifted(k):
            rot = pltpu.roll(u3, k, 1)
            above = pltpu.roll(before, k, 0)[None]
            if groups > 1:
                above = jnp.concatenate([above, rot[:-1]], axis=0)
            return jnp.where(sub < k, above, rot).reshape(ts, fc)

        out = (cb_ref[:, cols] + shifted(2) * cw_ref[0:1, cols] + shifted(1) * cw_ref[1:2, cols]
               + u * cw_ref[2:3, cols])
        tail = u[ts - CARRY_ROWS:, :]
        carry_ref[:, cols] = tail
        nconv_ref[0, :, cols] = tail
        return out

    acc = None
    u = up(0)
    acts = []

    def down(c_end):
        act = jnp.concatenate(acts, axis=1) if len(acts) > 1 else acts[0]
        part = _dot(act, wdn_ref[(c_end - len(acts)) * fc:c_end * fc, :])
        return part if acc is None else acc + part

    for c in range(nc):
        u_next = up(c + 1) if c + 1 < nc else None
        if len(acts) == 2:
            acc = down(c)
            acts = []
        g = conv(u[0], c * fc)
        v = conv(u[1], ff + c * fc)
        acts.append((g / (1.0 + jnp.exp(-g)) * v).astype(BF16))
        u = u_next
    y_ref[0] = _rms(x1 + down(nc), finw_ref[...])


def _mlp(x, od, osb, cinit, wod, wos, fnw, wup, cw, cb, wdn, finw, ts):
    b, s, d = x.shape
    ff = wdn.shape[0]
    half = od.shape[2]
    fc = 256
    const = lambda a: pl.BlockSpec(a.shape, lambda i, j: (0, 0))
    tok = lambda w: pl.BlockSpec((1, ts, w), lambda i, j: (i, j, 0))
    state = pl.BlockSpec((1, CARRY_ROWS, 2 * ff), lambda i, j: (i, 0, 0))
    return pl.pallas_call(
        functools.partial(_mlp_kernel, ts=ts, ff=ff, fc=fc),
        grid=(b, s // ts),
        in_specs=[tok(d), tok(half), tok(half), state, const(wod), const(wos), const(fnw), const(wup),
                  const(cw), const(cb), const(wdn), const(finw)],
        out_specs=[tok(d), state],
        out_shape=[jax.ShapeDtypeStruct((b, s, d), F32),
                   jax.ShapeDtypeStruct((b, CARRY_ROWS, 2 * ff), F32)],
        scratch_shapes=[pltpu.VMEM((CARRY_ROWS, 2 * ff), F32)],
        compiler_params=pltpu.CompilerParams(dimension_semantics=("arbitrary", "arbitrary"),
                                             vmem_limit_bytes=VMEM_LIMIT),
        name="mlp",
    )(x, od, osb, cinit, wod, wos, fnw, wup, cw, cb, wdn, finw)


def kernel(x_prompt, x_sample, cache_diff_k, cache_diff_v, cache_sb_k, cache_sb_v, state_conv, attn_norm_w, w_in, lambda_q1, lambda_k1, lambda_q2, lambda_k2, diff_subln_w, sb_norm_w, w_out, ffn_norm_w, w_up, conv_w, conv_b, w_down, rel_bias, final_norm_w):
    depth = w_in.shape[0]
    assert depth == 1, "single-layer step"
    bp, sp, d = x_prompt.shape
    bs, ss, _ = x_sample.shape
    past = cache_diff_k.shape[3]
    n_dh = cache_diff_k.shape[2]
    n_sh = cache_sb_k.shape[2]
    ff = w_down.shape[1]
    tile = ATT_TILE
    assert sp % tile == 0 and past % tile == 0 and tile % CHUNK == 0 and n_sh == 2 * n_dh
    assert ss % CARRY_ROWS == 0
    lam_init = 0.8 - 0.6 * math.exp(-0.3 * 0)

    loc = np.arange(tile)
    bkt_p = np.stack([_bucket_tiles(loc + tile, loc + tile), _bucket_tiles(loc + tile, loc)])
    far_bucket = int(_rel_bucket_np(np.array([-(tile + 1)]))[0])
    assert far_bucket == int(_rel_bucket_np(np.array([-(sp + past)]))[0])
    qpos_s = past + np.arange(ss)
    bkt_sp = _bucket_tiles(qpos_s, np.arange(past))
    bkt_sn = _bucket_tiles(qpos_s, qpos_s)

    row = lambda v: v.reshape(1, -1).astype(F32)
    bias_p, bias_sp, bias_sn, lam = _prep(
        rel_bias.astype(F32), row(lambda_q1[0]), row(lambda_k1[0]), row(lambda_q2[0]), row(lambda_k2[0]),
        jnp.asarray(bkt_p), jnp.asarray(bkt_sp), jnp.asarray(bkt_sn), lam_init, far_bucket)

    w_in_bf = w_in[0].astype(BF16)
    w_out_bf = w_out[0].astype(BF16)
    half = n_dh * LANES
    wod, wos = w_out_bf[:half], w_out_bf[half:]
    wup = w_up[0].astype(BF16)
    wdn = w_down[0].astype(BF16)
    anw, fnw, finw = row(attn_norm_w[0]), row(ffn_norm_w[0]), row(final_norm_w)
    subw = row(diff_subln_w[0]) * (1.0 - lam_init)
    nw2 = jnp.tile(row(sb_norm_w[0]), (1, 2))
    cw, cb = conv_w[0].astype(F32), row(conv_b[0])
    def suffix_matrix(n):
        m = np.triu(np.ones((n, n), np.float32), 1).T
        return jnp.asarray(np.concatenate([m, m], axis=0), BF16)

    tri, tri_s = suffix_matrix(tile), suffix_matrix(SB_SUB)

    def state8(st):
        return jnp.pad(st, ((0, 0), (CARRY_ROWS - st.shape[1], 0), (0, 0)))

    ts_p = min(512, sp)
    proj_p, p_dk, p_dv, p_sk, p_sv = _inproj(x_prompt, anw, w_in_bf, ts_p)
    od_p = _diff_attn(lam, proj_p, bias_p, subw, n_dh)
    os_p = _sb_attn(proj_p, tri, tri_s, nw2, n_dh, 3 * n_dh)
    y_p, cv_p = _mlp(x_prompt, od_p, os_p, jnp.zeros((bp, CARRY_ROWS, 2 * ff), F32),
                     wod, wos, fnw, wup, cw, cb, wdn, finw, min(1024, sp))

    proj_s, s_dk, s_dv, s_sk, s_sv = _inproj(x_sample, anw, w_in_bf, bs * ss, seqs=bs)
    od_s, os_s = _sample_attn(lam, proj_s, cache_diff_k, cache_diff_v,
                              jnp.swapaxes(cache_sb_k, 3, 4), jnp.swapaxes(cache_sb_v, 3, 4),
                              bias_sp, bias_sn, subw, nw2, tri, n_dh)
    y_s, cv_s = _mlp(x_sample, od_s, os_s, state8(state_conv[0]),
                     wod, wos, fnw, wup, cw, cb, wdn, finw, ss)

    tail = lambda cv: cv[None, :, CARRY_ROWS - 2:, :]
    return (y_p, y_s, p_dk, p_dv, p_sk, p_sv, tail(cv_p), s_dk, s_dv, s_sk, s_sv, tail(cv_s))
```

```python
import functools
import math

import numpy as np
import jax
import jax.numpy as jnp
from jax import lax
from jax.experimental import pallas as pl
from jax.experimental.pallas import tpu as pltpu

F32 = jnp.float32
BF16 = jnp.bfloat16

HEAD_DIM = 64
LANES = 128
CHUNK = 64
N_BUCKETS = 32
EPS = 1e-6
NEG = -1e30
LOG2E = 1.4426950408889634
SIGN_BIT = np.int32(-2 ** 31)
SB_ZERO_LOG2 = -160.0
ATT_TILE = 256
SB_SUB = ATT_TILE // 2
VMEM_LIMIT = 60 * 1024 * 1024


def _rel_bucket_np(rel):
    half = N_BUCKETS // 2
    max_exact = half // 2
    n = np.abs(rel)
    thresholds = (12, 16, 23, 32, 46, 64, 91)
    large = max_exact + sum((n >= t).astype(np.int64) for t in thresholds)
    return (np.where(rel > 0, half, 0) + np.where(n < max_exact, n, large)).astype(np.int32)


def _bucket_tiles(qpos, kpos):
    rel = kpos[None, :] - qpos[:, None]
    vis = (kpos[None, :] // CHUNK) <= (qpos[:, None] // CHUNK)
    return np.where(vis, _rel_bucket_np(rel), -1).astype(np.int32)


def _nt_dot(a, b):
    return lax.dot_general(a, b, (((1,), (1,)), ((), ())), preferred_element_type=F32)


def _dot(a, b):
    return jnp.dot(a, b, preferred_element_type=F32)


def _rms(x, w):
    return x * lax.rsqrt(jnp.mean(x * x, axis=-1, keepdims=True) + EPS) * w


def _prep_kernel(rb_ref, lq1_ref, lk1_ref, lq2_ref, lk2_ref, bp_ref, bsp_ref, bsn_ref,
                 bias_p_ref, bias_sp_ref, bias_sn_ref, lam_ref, *, lam_init, far_bucket):
    h = pl.program_id(0)
    pairs = ((bp_ref, bias_p_ref), (bsp_ref, bias_sp_ref), (bsn_ref, bias_sn_ref))
    for _, out in pairs:
        out[...] = jnp.full(out.shape, NEG, F32)

    def body(b, carry):
        val = rb_ref[b, h]
        for idx, out in pairs:
            out[0] = jnp.where(idx[...] == b, val, out[0])
        return carry

    lax.fori_loop(0, N_BUCKETS, body, 0)
    bias_p_ref[0] = bias_p_ref[0] - rb_ref[far_bucket, h]
    s1 = jnp.sum(lq1_ref[...] * lk1_ref[...], axis=-1, keepdims=True)
    s2 = jnp.sum(lq2_ref[...] * lk2_ref[...], axis=-1, keepdims=True)
    lam_ref[...] = jnp.exp(s1) - jnp.exp(s2) + lam_init


def _prep(rel_bias, lq1, lk1, lq2, lk2, bkt_p, bkt_sp, bkt_sn, lam_init, far_bucket):
    nh = rel_bias.shape[1]
    full = lambda a: pl.BlockSpec(a.shape, lambda h: (0,) * a.ndim)
    head = lambda a: pl.BlockSpec((1,) + a.shape, lambda h: (h,) + (0,) * a.ndim)
    vec = pl.BlockSpec((1, HEAD_DIM), lambda h: (0, 0))
    return pl.pallas_call(
        functools.partial(_prep_kernel, lam_init=lam_init, far_bucket=far_bucket),
        grid=(nh,),
        in_specs=[pl.BlockSpec(memory_space=pltpu.SMEM), vec, vec, vec, vec,
                  full(bkt_p), full(bkt_sp), full(bkt_sn)],
        out_specs=[head(bkt_p), head(bkt_sp), head(bkt_sn), pl.BlockSpec((1, 1), lambda h: (0, 0))],
        out_shape=[jax.ShapeDtypeStruct((nh,) + bkt_p.shape, F32),
                   jax.ShapeDtypeStruct((nh,) + bkt_sp.shape, F32),
                   jax.ShapeDtypeStruct((nh,) + bkt_sn.shape, F32),
                   jax.ShapeDtypeStruct((1, 1), F32)],
        name="prep",
    )(rel_bias, lq1, lk1, lq2, lk2, bkt_p, bkt_sp, bkt_sn)


def _inproj_kernel(x_ref, nw_ref, w_ref, proj_ref, dk_ref, dv_ref, sk_ref, sv_ref, *, width, seqs):
    h = _rms(x_ref[0], nw_ref[...]).astype(BF16)
    rows = h.shape[0] // seqs
    for g in (4, 5, 1, 2, 3, 0):
        acc = _dot(h, w_ref[:, g * width:(g + 1) * width])
        proj_ref[0, :, g * width:(g + 1) * width] = acc.astype(BF16)
        out, hd = {1: (dk_ref, LANES), 2: (dv_ref, LANES), 4: (sk_ref, HEAD_DIM), 5: (sv_ref, HEAD_DIM)}.get(
            g, (None, None))
        if out is not None:
            for q in range(seqs):
                for hh in range(width // hd):
                    out[0, q, hh] = acc[q * rows:(q + 1) * rows, hh * hd:(hh + 1) * hd]


def _inproj(x, nw, w_in_bf, ts, seqs=1):
    b, s, d = x.shape
    assert seqs == 1 or ts == seqs * s
    cols = w_in_bf.shape[1]
    width = cols // 6
    n_dh, n_sh = width // LANES, width // HEAD_DIM
    xg = x.reshape(b // seqs, seqs * s, d)
    kv = lambda nh, hd: pl.BlockSpec((1, seqs, nh, ts // seqs, hd), lambda i, j: (0, i, 0, j, 0))
    outs = pl.pallas_call(
        functools.partial(_inproj_kernel, width=width, seqs=seqs),
        grid=(b // seqs, seqs * s // ts),
        in_specs=[pl.BlockSpec((1, ts, d), lambda i, j: (i, j, 0)),
                  pl.BlockSpec((1, d), lambda i, j: (0, 0)),
                  pl.BlockSpec((d, cols), lambda i, j: (0, 0))],
        out_specs=[pl.BlockSpec((1, ts, cols), lambda i, j: (i, j, 0)),
                   kv(n_dh, LANES), kv(n_dh, LANES), kv(n_sh, HEAD_DIM), kv(n_sh, HEAD_DIM)],
        out_shape=[jax.ShapeDtypeStruct(xg.shape[:2] + (cols,), BF16),
                   jax.ShapeDtypeStruct((1, b, n_dh, s, LANES), F32),
                   jax.ShapeDtypeStruct((1, b, n_dh, s, LANES), F32),
                   jax.ShapeDtypeStruct((1, b, n_sh, s, HEAD_DIM), F32),
                   jax.ShapeDtypeStruct((1, b, n_sh, s, HEAD_DIM), F32)],
        compiler_params=pltpu.CompilerParams(vmem_limit_bytes=VMEM_LIMIT),
        name="inproj",
    )(xg, nw, w_in_bf)
    return (outs[0].reshape(b, s, cols),) + tuple(outs[1:])


def _diff_finish(a1, l1, a2, l2, lam, subw):
    o = a1 / l1 - lam * (a2 / l2)
    return _rms(o, subw)


def _pair_rms(o, lane, w):
    o2 = o * o
    tot = jnp.sum(o2, axis=-1, keepdims=True)
    lo = jnp.sum(jnp.where(lane < HEAD_DIM, o2, 0.0), axis=-1, keepdims=True)
    ms = jnp.where(lane < HEAD_DIM, lo, tot - lo) * (1.0 / HEAD_DIM)
    return o * lax.rsqrt(ms + EPS) * w


def _diff_kernel(lam_ref, q_ref, k_ref, v_ref, bias_ref, subw_ref, o_ref, *, tile, nq):
    lam = lam_ref[0, 0]
    lane = lax.broadcasted_iota(jnp.int32, (tile, LANES), 1)
    subw = subw_ref[...]
    rowmax = lambda t: jnp.max(t, axis=-1, keepdims=True)
    rowsum = lambda t: jnp.sum(t, axis=-1, keepdims=True)

    def scores(qi):
        q = q_ref[0, qi * tile:(qi + 1) * tile, :] * jnp.asarray(HEAD_DIM ** -0.5, BF16)
        zero = jnp.zeros_like(q)
        qq = jnp.concatenate([jnp.where(lane < HEAD_DIM, q, zero), jnp.where(lane >= HEAD_DIM, q, zero)], axis=0)
        return _nt_dot(qq, k_ref[0, :(qi + 1) * tile, :])

    def probs(qi, s):
        parts = [s[:, qi * tile:] + bias_ref[0, 0]]
        if qi >= 1:
            parts.append(s[:, (qi - 1) * tile:qi * tile] + bias_ref[0, 1])
        if qi >= 2:
            parts.append(s[:, :(qi - 1) * tile])
        m = functools.reduce(jnp.maximum, [rowmax(t) for t in parts])
        ps = [jnp.exp(t - m) for t in parts]
        l = functools.reduce(jnp.add, [rowsum(t) for t in ps])
        return jnp.concatenate([t.astype(BF16) for t in reversed(ps)], axis=1), l

    pending = {}
    for t in range(nq + 1):
        if t < nq:
            pending[t] = scores(t)
        if t >= 1:
            qi = t - 1
            s = pending.pop(qi)
            (p1, l1), (p2, l2) = probs(qi, s[:tile]), probs(qi, s[tile:])
            a = _dot(jnp.concatenate([p1, p2], axis=0), v_ref[0, :(qi + 1) * tile, :])
            o_ref[0, qi * tile:(qi + 1) * tile, :] = _diff_finish(
                a[:tile], l1, a[tile:], l2, lam, subw).astype(BF16)


def _diff_attn(lam, proj, bias_p, subw, n_heads):
    b, s, _ = proj.shape
    tile = ATT_TILE
    blk = lambda off: pl.BlockSpec((1, s, LANES), lambda i, h: (i, 0, off + h))
    return pl.pallas_call(
        functools.partial(_diff_kernel, tile=tile, nq=s // tile),
        grid=(b, n_heads),
        in_specs=[pl.BlockSpec(memory_space=pltpu.SMEM),
                  blk(0), blk(n_heads), blk(2 * n_heads),
                  pl.BlockSpec((1, 2, tile, tile), lambda i, h: (h, 0, 0, 0)),
                  pl.BlockSpec((1, LANES), lambda i, h: (0, 0))],
        out_specs=pl.BlockSpec((1, s, LANES), lambda i, h: (i, 0, h)),
        out_shape=jax.ShapeDtypeStruct((b, s, n_heads * LANES), BF16),
        compiler_params=pltpu.CompilerParams(vmem_limit_bytes=VMEM_LIMIT),
        name="diff_attn",
    )(lam, proj, proj, proj, bias_p, subw)


def _sb_logits(z2, tiles, run=None, total=False):
    neg_abs = lax.bitcast_convert_type(lax.bitcast_convert_type(z2, jnp.int32) | SIGN_BIT, F32)
    sp = jnp.log2(1.0 + jnp.exp2(neg_abs))
    lb = jnp.minimum(z2, 0.0) - sp
    l1m = lb - z2
    stops = np.cumsum([0] + [width for width, _, _ in tiles])
    xs = [None] * len(tiles)
    for j in range(len(tiles) - 1, -1, -1):
        width, tri2, mask = tiles[j]
        cols = slice(int(stops[j]), int(stops[j + 1]))
        lj = l1m[:, cols]
        if mask is not None:
            lj = jnp.where(mask, lj, 0.0)
        hi = lj.astype(BF16)
        lo = (lj - hi.astype(F32)).astype(BF16)
        if width % LANES == 0:
            suffix = _dot(jnp.concatenate([hi, lo], axis=1), tri2)
        else:
            suffix = _dot(hi, tri2) + _dot(lo, tri2)
        x = lb[:, cols] + suffix
        xs[j] = x if run is None else x + run
        if j > 0 or total:
            rs = jnp.sum(lj, axis=-1, keepdims=True)
            run = rs if run is None else run + rs
    return xs, run


def _sb_weights(xs, tiles):
    ws = []
    for x, (_, _, mask) in zip(xs, tiles):
        w = jnp.exp2(x)
        ws.append((w if mask is None else jnp.where(mask, w, 0.0)).astype(BF16))
    return jnp.concatenate(ws, axis=1) if len(ws) > 1 else ws[0]


def _strict_causal(n):
    return lax.broadcasted_iota(jnp.int32, (n, n), 1) < lax.broadcasted_iota(jnp.int32, (n, n), 0)


def _sb_kernel(q_ref, k_ref, v_ref, tri_ref, tris_ref, nw_ref, o_ref, *, tile, sub, nq):
    tri, tri_s = tri_ref[...], tris_ref[...]
    cmask = _strict_causal(tile)
    row2 = lax.broadcasted_iota(jnp.int32, (2 * sub, sub), 0) & (sub - 1)
    cmask_s = lax.broadcasted_iota(jnp.int32, (2 * sub, sub), 1) < row2
    nw = nw_ref[...]
    scale = jnp.asarray(HEAD_DIM ** -0.5, BF16)

    def head_q(q, half):
        lane = lax.broadcasted_iota(jnp.int32, q.shape, 1)
        in_half = (lane >= half * HEAD_DIM) & (lane < (half + 1) * HEAD_DIM)
        return jnp.where(in_half, q, jnp.zeros_like(q))

    def finish(halves):
        lane = lax.broadcasted_iota(jnp.int32, halves[0].shape, 1)
        o = jnp.where(lane < HEAD_DIM, halves[0], halves[1])
        return _pair_rms(o, lane, nw).astype(BF16)

    def window(j):
        end = (j + 1) * sub
        start = max(end - sub - tile, 0)
        before = end - sub - start
        tiles = [(before, tri if before == tile else tri_s, None)] if before else []
        return start, end, tiles + [(sub, tri_s, cmask_s)]

    n_sub = nq * tile // sub
    z2s, xss = {}, {}
    worst = None
    for t in range(n_sub + 2):
        if t < n_sub:
            start, end, _ = window(t)
            q = q_ref[0, t * sub:(t + 1) * sub, :] * scale
            qq = jnp.concatenate([head_q(q, 0), head_q(q, 1)], axis=0)
            z2s[t] = _nt_dot(qq, k_ref[0, start:end, :]) * LOG2E
        if 1 <= t <= n_sub:
            start, _, tiles = window(t - 1)
            xss[t - 1], run = _sb_logits(z2s.pop(t - 1), tiles, None, total=start > 0)
            if start > 0:
                worst = run if worst is None else jnp.maximum(worst, run)
        if t >= 2:
            j = t - 2
            start, end, tiles = window(j)
            a = _dot(_sb_weights(xss.pop(j), tiles), v_ref[0, start:end, :])
            o_ref[0, j * sub:(j + 1) * sub, :] = finish([a[:sub], a[sub:]])

    if worst is None:
        return

    @pl.when(jnp.max(worst) >= SB_ZERO_LOG2)
    def _():
        def q_body(qi, carry):
            q0 = pl.multiple_of(qi * tile, tile)
            q = q_ref[0, pl.ds(q0, tile), :] * scale
            diag, plain = [(tile, tri, cmask)], [(tile, tri, None)]
            full = []
            for half in range(2):
                qm = head_q(q, half)
                xs, run = _sb_logits(_nt_dot(qm, k_ref[0, pl.ds(q0, tile), :]) * LOG2E, diag, None, total=True)
                acc = _dot(_sb_weights(xs, diag), v_ref[0, pl.ds(q0, tile), :])

                def kb_body(it, c):
                    acc, run = c
                    k0 = pl.multiple_of((qi - 1 - it) * tile, tile)
                    xs, run = _sb_logits(_nt_dot(qm, k_ref[0, pl.ds(k0, tile), :]) * LOG2E, plain, run, total=True)
                    return acc + _dot(_sb_weights(xs, plain), v_ref[0, pl.ds(k0, tile), :]), run

                acc, _ = lax.fori_loop(0, qi, kb_body, (acc, run))
                full.append(acc)
            o_ref[0, pl.ds(q0, tile), :] = finish(full)
            return carry

        lax.fori_loop(1, nq, q_body, 0)


def _sb_attn(proj, tri, tri_s, nw2, n_pairs, off):
    b, s, _ = proj.shape
    tile = ATT_TILE
    blk = lambda o: pl.BlockSpec((1, s, LANES), lambda i, p: (i, 0, o + p))
    return pl.pallas_call(
        functools.partial(_sb_kernel, tile=tile, sub=SB_SUB, nq=s // tile),
        grid=(b, n_pairs),
        in_specs=[blk(off), blk(off + n_pairs), blk(off + 2 * n_pairs),
                  pl.BlockSpec(tri.shape, lambda i, p: (0, 0)),
                  pl.BlockSpec(tri_s.shape, lambda i, p: (0, 0)),
                  pl.BlockSpec((1, LANES), lambda i, p: (0, 0))],
        out_specs=pl.BlockSpec((1, s, LANES), lambda i, p: (i, 0, p)),
        out_shape=jax.ShapeDtypeStruct((b, s, n_pairs * LANES), BF16),
        compiler_params=pltpu.CompilerParams(vmem_limit_bytes=VMEM_LIMIT),
        name="sb_attn",
    )(proj, proj, proj, tri, tri_s, nw2)


def _sample_attn_kernel(lam_ref, qd_ref, kd_ref, vd_ref, qs_ref, ks_ref, vs_ref,
                        cdk_ref, cdv_ref, csk_ref, csv_ref, bsp_ref, bsn_ref,
                        subw_ref, nw_ref, tri_ref, od_ref, os_ref, *, sq, past, tile, heads):
    lam = lam_ref[0, 0]
    scale = jnp.asarray(HEAD_DIM ** -0.5, BF16)
    lane = lax.broadcasted_iota(jnp.int32, (sq, LANES), 1)
    cmask = _strict_causal(sq)
    tri2 = tri_ref[...]
    tri_new = tri2[:sq, :sq]
    halves_of = [slice(hh * HEAD_DIM, (hh + 1) * HEAD_DIM) for hh in range(2)]

    for h in range(heads):
        cols = slice(h * LANES, (h + 1) * LANES)

        q = qd_ref[0][:, cols] * scale
        kc = cdk_ref[0, 0, h].astype(BF16)
        vc = cdv_ref[0, 0, h].astype(BF16)
        kn = kd_ref[0][:, cols]
        vn = vd_ref[0][:, cols]
        outs = []
        for qm in (jnp.where(lane < HEAD_DIM, q, jnp.zeros_like(q)),
                   jnp.where(lane >= HEAD_DIM, q, jnp.zeros_like(q))):
            sc = _nt_dot(qm, kc) + bsp_ref[h]
            sn = _nt_dot(qm, kn) + bsn_ref[h]
            m = jnp.maximum(jnp.max(sc, axis=-1, keepdims=True), jnp.max(sn, axis=-1, keepdims=True))
            pc = jnp.exp(sc - m)
            pn = jnp.exp(sn - m)
            l = jnp.sum(pc, axis=-1, keepdims=True) + jnp.sum(pn, axis=-1, keepdims=True)
            outs.append((_dot(pc.astype(BF16), vc) + _dot(pn.astype(BF16), vn), l))
        (a1, l1), (a2, l2) = outs
        od_ref[0, :, cols] = _diff_finish(a1, l1, a2, l2, lam, subw_ref[...]).astype(BF16)

        qs = qs_ref[0][:, cols] * scale
        ks = ks_ref[0][:, cols]
        vs = vs_ref[0][:, cols]
        z_new = [_nt_dot(qs[:, sl], ks[:, sl]) * LOG2E for sl in halves_of]
        z_old = [_dot(qs[:, sl], csk_ref[0, 0, 2 * h + hh].astype(BF16)) * LOG2E
                 for hh, sl in enumerate(halves_of)]
        halves = []
        for hh, sl in enumerate(halves_of):
            new, old = [(sq, tri_new, cmask)], [(tile, tri2, None)] * (past // tile)
            x_new, run = _sb_logits(z_new[hh], new, None, total=True)
            x_old, _ = _sb_logits(z_old[hh], old, run)
            halves.append(_dot(_sb_weights(x_new, new), vs[:, sl])
                          + _nt_dot(_sb_weights(x_old, old), csv_ref[0, 0, 2 * h + hh].astype(BF16)))
        o = jnp.concatenate(halves, axis=-1)
        os_ref[0, :, cols] = _pair_rms(o, lane, nw_ref[...]).astype(BF16)


def _sample_attn(lam, proj, cdk, cdv, csk, csv, bias_sp, bias_sn, subw, nw2, tri, n_dh):
    b, sq, _ = proj.shape
    past = cdk.shape[3]
    tile = ATT_TILE
    wide = n_dh * LANES
    blk = lambda off: pl.BlockSpec((1, sq, wide), lambda i: (i, 0, off))
    cd = pl.BlockSpec((1, 1, n_dh, past, LANES), lambda i: (0, i, 0, 0, 0))
    cs = pl.BlockSpec((1, 1, 2 * n_dh, HEAD_DIM, past), lambda i: (0, i, 0, 0, 0))
    full = lambda a: pl.BlockSpec(a.shape, lambda i: (0,) * a.ndim)
    out = pl.BlockSpec((1, sq, wide), lambda i: (i, 0, 0))
    return pl.pallas_call(
        functools.partial(_sample_attn_kernel, sq=sq, past=past, tile=tile, heads=n_dh),
        grid=(b,),
        in_specs=[pl.BlockSpec(memory_space=pltpu.SMEM),
                  blk(0), blk(1), blk(2), blk(3), blk(4), blk(5),
                  cd, cd, cs, cs, full(bias_sp), full(bias_sn), full(subw), full(nw2), full(tri)],
        out_specs=[out, out],
        out_shape=[jax.ShapeDtypeStruct((b, sq, wide), BF16)] * 2,
        compiler_params=pltpu.CompilerParams(vmem_limit_bytes=VMEM_LIMIT),
        name="sample_attn",
    )(lam, proj, proj, proj, proj, proj, proj, cdk, cdv, csk, csv, bias_sp, bias_sn, subw, nw2, tri)


CARRY_ROWS = 8


def _mlp_kernel(x_ref, od_ref, os_ref, cinit_ref, wod_ref, wos_ref, fnw_ref, wup_ref, cw_ref, cb_ref,
                wdn_ref, finw_ref, y_ref, nconv_ref, carry_ref, *, ts, ff, fc):
    @pl.when(pl.program_id(1) == 0)
    def _():
        carry_ref[...] = cinit_ref[0]

    mix = jnp.concatenate([od_ref[0], os_ref[0]], axis=1)
    x1 = x_ref[0] + _dot(mix, jnp.concatenate([wod_ref[...], wos_ref[...]], axis=0))
    h2 = _rms(x1, fnw_ref[...]).astype(BF16)
    groups = ts // CARRY_ROWS
    sub = lax.broadcasted_iota(jnp.int32, (groups, CARRY_ROWS, fc), 1)
    nc = ff // fc

    def up(c):
        return tuple(_dot(h2, wup_ref[:, c0:c0 + fc]) for c0 in (c * fc, ff + c * fc))

    def conv(u, c0):
        cols = slice(c0, c0 + fc)
        u3 = u.reshape(groups, CARRY_ROWS, fc)
        before = carry_ref[:, cols]

        def shifted(k):
            rot = pltpu.roll(u3, k, 1)
            above = pltpu.roll(before, k, 0)[None]
            if groups > 1:
                above = jnp.concatenate([above, rot[:-1]], axis=0)
            return jnp.where(sub < k, above, rot).reshape(ts, fc)

        out = (cb_ref[:, cols] + shifted(2) * cw_ref[0:1, cols] + shifted(1) * cw_ref[1:2, cols]
               + u * cw_ref[2:3, cols])
        tail = u[ts - CARRY_ROWS:, :]
        carry_ref[:, cols] = tail
        nconv_ref[0, :, cols] = tail
        return out

    acc = None
    u = up(0)
    acts = []

    def down(c_end):
        act = jnp.concatenate(acts, axis=1) if len(acts) > 1 else acts[0]
        part = _dot(act, wdn_ref[(c_end - len(acts)) * fc:c_end * fc, :])
        return part if acc is None else acc + part

    for c in range(nc):
        u_next = up(c + 1) if c + 1 < nc else None
        if len(acts) == 2:
            acc = down(c)
            acts = []
        g = conv(u[0], c * fc)
        v = conv(u[1], ff + c * fc)
        acts.append((g / (1.0 + jnp.exp(-g)) * v).astype(BF16))
        u = u_next
    y_ref[0] = _rms(x1 + down(nc), finw_ref[...])


def _mlp(x, od, osb, cinit, wod, wos, fnw, wup, cw, cb, wdn, finw, ts):
    b, s, d = x.shape
    ff = wdn.shape[0]
    half = od.shape[2]
    fc = 256
    const = lambda a: pl.BlockSpec(a.shape, lambda i, j: (0, 0))
    tok = lambda w: pl.BlockSpec((1, ts, w), lambda i, j: (i, j, 0))
    state = pl.BlockSpec((1, CARRY_ROWS, 2 * ff), lambda i, j: (i, 0, 0))
    return pl.pallas_call(
        functools.partial(_mlp_kernel, ts=ts, ff=ff, fc=fc),
        grid=(b, s // ts),
        in_specs=[tok(d), tok(half), tok(half), state, const(wod), const(wos), const(fnw), const(wup),
                  const(cw), const(cb), const(wdn), const(finw)],
        out_specs=[tok(d), state],
        out_shape=[jax.ShapeDtypeStruct((b, s, d), F32),
                   jax.ShapeDtypeStruct((b, CARRY_ROWS, 2 * ff), F32)],
        scratch_shapes=[pltpu.VMEM((CARRY_ROWS, 2 * ff), F32)],
        compiler_params=pltpu.CompilerParams(dimension_semantics=("arbitrary", "arbitrary"),
                                             vmem_limit_bytes=VMEM_LIMIT),
        name="mlp",
    )(x, od, osb, cinit, wod, wos, fnw, wup, cw, cb, wdn, finw)


def kernel(x_prompt, x_sample, cache_diff_k, cache_diff_v, cache_sb_k, cache_sb_v, state_conv, attn_norm_w, w_in, lambda_q1, lambda_k1, lambda_q2, lambda_k2, diff_subln_w, sb_norm_w, w_out, ffn_norm_w, w_up, conv_w, conv_b, w_down, rel_bias, final_norm_w):
    depth = w_in.shape[0]
    assert depth == 1, "single-layer step"
    bp, sp, d = x_prompt.shape
    bs, ss, _ = x_sample.shape
    past = cache_diff_k.shape[3]
    n_dh = cache_diff_k.shape[2]
    n_sh = cache_sb_k.shape[2]
    ff = w_down.shape[1]
    tile = ATT_TILE
    assert sp % tile == 0 and past % tile == 0 and tile % CHUNK == 0 and n_sh == 2 * n_dh
    assert ss % CARRY_ROWS == 0
    lam_init = 0.8 - 0.6 * math.exp(-0.3 * 0)

    loc = np.arange(tile)
    bkt_p = np.stack([_bucket_tiles(loc + tile, loc + tile), _bucket_tiles(loc + tile, loc)])
    far_bucket = int(_rel_bucket_np(np.array([-(tile + 1)]))[0])
    assert far_bucket == int(_rel_bucket_np(np.array([-(sp + past)]))[0])
    qpos_s = past + np.arange(ss)
    bkt_sp = _bucket_tiles(qpos_s, np.arange(past))
    bkt_sn = _bucket_tiles(qpos_s, qpos_s)

    row = lambda v: v.reshape(1, -1).astype(F32)
    bias_p, bias_sp, bias_sn, lam = _prep(
        rel_bias.astype(F32), row(lambda_q1[0]), row(lambda_k1[0]), row(lambda_q2[0]), row(lambda_k2[0]),
        jnp.asarray(bkt_p), jnp.asarray(bkt_sp), jnp.asarray(bkt_sn), lam_init, far_bucket)

    w_in_bf = w_in[0].astype(BF16)
    w_out_bf = w_out[0].astype(BF16)
    half = n_dh * LANES
    wod, wos = w_out_bf[:half], w_out_bf[half:]
    wup = w_up[0].astype(BF16)
    wdn = w_down[0].astype(BF16)
    anw, fnw, finw = row(attn_norm_w[0]), row(ffn_norm_w[0]), row(final_norm_w)
    subw = row(diff_subln_w[0]) * (1.0 - lam_init)
    nw2 = jnp.tile(row(sb_norm_w[0]), (1, 2))
    cw, cb = conv_w[0].astype(F32), row(conv_b[0])
    def suffix_matrix(n):
        m = np.triu(np.ones((n, n), np.float32), 1).T
        return jnp.asarray(np.concatenate([m, m], axis=0), BF16)

    tri, tri_s = suffix_matrix(tile), suffix_matrix(SB_SUB)

    def state8(st):
        return jnp.pad(st, ((0, 0), (CARRY_ROWS - st.shape[1], 0), (0, 0)))

    ts_p = min(512, sp)
    proj_p, p_dk, p_dv, p_sk, p_sv = _inproj(x_prompt, anw, w_in_bf, ts_p)
    od_p = _diff_attn(lam, proj_p, bias_p, subw, n_dh)
    os_p = _sb_attn(proj_p, tri, tri_s, nw2, n_dh, 3 * n_dh)
    y_p, cv_p = _mlp(x_prompt, od_p, os_p, jnp.zeros((bp, CARRY_ROWS, 2 * ff), F32),
                     wod, wos, fnw, wup, cw, cb, wdn, finw, min(1024, sp))

    proj_s, s_dk, s_dv, s_sk, s_sv = _inproj(x_sample, anw, w_in_bf, bs * ss, seqs=bs)
    od_s, os_s = _sample_attn(lam, proj_s, cache_diff_k, cache_diff_v,
                              jnp.swapaxes(cache_sb_k, 3, 4), jnp.swapaxes(cache_sb_v, 3, 4),
                              bias_sp, bias_sn, subw, nw2, tri, n_dh)
    y_s, cv_s = _mlp(x_sample, od_s, os_s, state8(state_conv[0]),
                     wod, wos, fnw, wup, cw, cb, wdn, finw, ss)

    tail = lambda cv: cv[None, :, CARRY_ROWS - 2:, :]
    return (y_p, y_s, p_dk, p_dv, p_sk, p_sv, tail(cv_p), s_dk, s_dv, s_sk, s_sv, tail(cv_s))
```
